```python
import math
import jax
import jax.numpy as jnp
from jax import lax
import numpy as np

D_MODEL = 1024
BATCH = 2
SEQ = 8192
DEPTH = 1
DEC_BATCH = 32
DEC_SEQ = 4
PAST_LEN = 16384
PAGE_SIZE = 128

GLA_HEADS = 4
GLA_QK = D_MODEL // 2
GLA_V = D_MODEL
GLA_DK = GLA_QK // GLA_HEADS
GLA_DV = GLA_V // GLA_HEADS
GLA_RANK = 16
GLA_TAU = 16.0
GLA_CHUNK = 32
DA_WINDOWS = (128, 512, 2048)
DA_DILATIONS = (1, 4, 16)
N_DA_GROUPS = 3
DA_HEADS = 8
DA_HEAD_DIM = 64
DA_WIDTH = DA_HEADS * DA_HEAD_DIM
N_EXPERTS = 32
TOP_K = 4
D_FF = D_MODEL
SWIGLU_LIMIT = 7.0
SWIGLU_ALPHA = 1.702
MOE_BLOCK = 128
PLE_DIM = 256
EPS = 1e-6
IN_WIDTHS = (GLA_QK, GLA_QK, GLA_V, GLA_V, GLA_RANK) + (DA_WIDTH,) * (3 * N_DA_GROUPS) + (D_MODEL, D_MODEL)
D_IN = sum(IN_WIDTHS)

kernel_name = "hybrid_gla_dilated_moe_step"


def rms_norm(x, g=None):
    x32 = x.astype(jnp.float32)
    y = x32 * lax.rsqrt(jnp.mean(x32 * x32, axis=-1, keepdims=True) + EPS)
    if g is not None:
        y = y * g.astype(jnp.float32)
    return y.astype(x.dtype)


def gla_chunked(q, k, v, log_a, s0, chunk):
    b, t, h, _ = q.shape
    dv = v.shape[-1]
    n = t // chunk

    def blocks(a):
        return a.astype(jnp.float32).reshape(b, n, chunk, h, a.shape[-1]).transpose(1, 0, 3, 2, 4)

    qc, kc, vc, lc = blocks(q), blocks(k), blocks(v), blocks(log_a)
    cum = jnp.cumsum(lc, axis=-2)
    last = cum[..., -1:, :]
    q_dec = qc * jnp.exp(cum)
    k_inv = kc * jnp.exp(-cum)
    k_end = kc * jnp.exp(last - cum)
    causal = jnp.arange(chunk)[:, None] >= jnp.arange(chunk)[None, :]
    att = jnp.where(causal, jnp.einsum("nbhtk,nbhsk->nbhts", q_dec, k_inv), 0.0)
    o_intra = jnp.einsum("nbhts,nbhsv->nbhtv", att, vc)
    u = jnp.einsum("nbhsk,nbhsv->nbhkv", k_end, vc)
    decay = jnp.exp(last[..., 0, :])

    def step(s, xs):
        q_n, u_n, d_n = xs
        o_n = jnp.einsum("bhtk,bhkv->bhtv", q_n, s)
        return d_n[..., None] * s + u_n, o_n

    s_fin, o_inter = lax.scan(step, s0.astype(jnp.float32), (q_dec, u, decay))
    o = (o_intra + o_inter).transpose(1, 0, 3, 2, 4).reshape(b, t, h, dv)
    return o, s_fin


def dilated_attn_prompt(q, k, v, dil, span):
    b, s, h, dh = q.shape
    L = s // dil
    nb = -(-L // span)
    pad = nb * span - L

    def sub(a):
        return a.reshape(b, L, dil, h, dh).transpose(0, 2, 1, 3, 4)

    qs = jnp.pad(sub(q), ((0, 0), (0, 0), (0, pad), (0, 0), (0, 0))).reshape(b, dil, nb, span, h, dh)
    kpad = ((0, 0), (0, 0), (span, pad), (0, 0), (0, 0))
    ks = jnp.pad(sub(k), kpad).reshape(b, dil, nb + 1, span, h, dh)
    vs = jnp.pad(sub(v), kpad).reshape(b, dil, nb + 1, span, h, dh)
    kb = jnp.concatenate([ks[:, :, :-1], ks[:, :, 1:]], axis=3)
    vb = jnp.concatenate([vs[:, :, :-1], vs[:, :, 1:]], axis=3)
    scores = jnp.einsum("brnqhd,brnkhd->brnhqk", qs, kb, preferred_element_type=jnp.float32) * (dh ** -0.5)
    qi = jnp.arange(span)
    kj = jnp.arange(2 * span)
    dist = span + qi[:, None] - kj[None, :]
    key_idx = jnp.arange(nb)[:, None, None] * span - span + kj[None, None, :]
    valid = (dist >= 0)[None] & (dist <= span)[None] & (key_idx >= 0)
    scores = jnp.where(valid[None, None, :, None], scores, -jnp.inf)
    lse = jax.nn.logsumexp(scores, axis=-1)
    p = jnp.exp(scores - lse[..., None])
    o = jnp.einsum("brnhqk,brnkhd->brnqhd", p.astype(v.dtype), vb)
    o = o.reshape(b, dil, nb * span, h, dh)[:, :, :L].transpose(0, 2, 1, 3, 4).reshape(b, s, h, dh)
    lse = lse.transpose(0, 1, 2, 4, 3).reshape(b, dil, nb * span, h)[:, :, :L]
    lse = lse.transpose(0, 2, 1, 3).reshape(b, s, h)
    return o, lse


def dilated_attn_sample(q, k, v, buf, dil, span):
    lb = buf.shape[1]
    t, dh = q.shape[1], q.shape[-1]
    kv_all = jnp.concatenate([buf, jnp.stack([k, v], axis=2).astype(buf.dtype)], axis=1)
    idx = lb + jnp.arange(t)[:, None] - dil * jnp.arange(span + 1)[None, :]
    valid = idx >= 0
    g = kv_all[:, jnp.maximum(idx, 0)]
    scores = jnp.einsum("bthd,btjhd->bhtj", q, g[:, :, :, 0], preferred_element_type=jnp.float32) * (dh ** -0.5)
    scores = jnp.where(valid[None, None], scores, -jnp.inf)
    lse = jax.nn.logsumexp(scores, axis=-1)
    p = jnp.exp(scores - lse[..., None])
    o = jnp.einsum("bhtj,btjhd->bthd", p.astype(q.dtype), g[:, :, :, 1])
    return o, lse.transpose(0, 2, 1), kv_all[:, -lb:]


def token_mixer(xn, s_gla, da_bufs, chunk, w_in, w_gla_alpha, b_gla_alpha, gla_norm_g, w_gla_out,
                qk_norm_q, qk_norm_k, w_da_out, w_o):
    b, t, _ = xn.shape
    split_points = [int(c) for c in np.cumsum(IN_WIDTHS)[:-1]]
    parts = jnp.split(xn @ w_in, split_points, axis=-1)
    gq, gk, gv, gr, glr = parts[0], parts[1], parts[2], parts[3], parts[4]
    da_parts = parts[5:5 + 3 * N_DA_GROUPS]
    gate_a, gate_b = parts[-2], parts[-1]

    log_a = jax.nn.log_sigmoid((glr @ w_gla_alpha + b_gla_alpha).astype(jnp.float32)) / GLA_TAU
    o_gla, s_new = gla_chunked(
        gq.reshape(b, t, GLA_HEADS, GLA_DK) * (GLA_DK ** -0.5),
        gk.reshape(b, t, GLA_HEADS, GLA_DK),
        gv.reshape(b, t, GLA_HEADS, GLA_DV),
        log_a.reshape(b, t, GLA_HEADS, GLA_DK), s_gla, chunk)
    o_gla = rms_norm(o_gla, gla_norm_g) * jax.nn.silu(gr.reshape(b, t, GLA_HEADS, GLA_DV).astype(jnp.float32))
    y_a = o_gla.astype(xn.dtype).reshape(b, t, GLA_V) @ w_gla_out

    outs, lses, new_bufs = [], [], []
    for g in range(N_DA_GROUPS):
        win, dil = DA_WINDOWS[g], DA_DILATIONS[g]
        span = win // dil
        q = rms_norm(da_parts[3 * g].reshape(b, t, DA_HEADS, DA_HEAD_DIM), qk_norm_q[g])
        k = rms_norm(da_parts[3 * g + 1].reshape(b, t, DA_HEADS, DA_HEAD_DIM), qk_norm_k[g])
        v = da_parts[3 * g + 2].reshape(b, t, DA_HEADS, DA_HEAD_DIM)
        if da_bufs is None:
            o_g, lse_g = dilated_attn_prompt(q, k, v, dil, span)
            buf_g = jnp.stack([k, v], axis=2)[:, -min(win, t):]
        else:
            o_g, lse_g, buf_g = dilated_attn_sample(q, k, v, da_bufs[g], dil, span)
        outs.append(o_g)
        lses.append(lse_g)
        new_bufs.append(buf_g)
    wts = jax.nn.softmax(jnp.stack(lses, axis=0), axis=0)
    o_da = jnp.sum(wts[..., None] * jnp.stack(outs, axis=0).astype(jnp.float32), axis=0)
    y_b = o_da.astype(xn.dtype).reshape(b, t, DA_WIDTH) @ w_da_out

    y = (jax.nn.sigmoid(gate_a) * y_a + jax.nn.sigmoid(gate_b) * y_b) @ w_o
    return y, s_new, new_bufs


def moe(xn, w_router, b_router, w_exp_in, b_exp_in, w_exp_out, b_exp_out):
    lead = xn.shape[:-1]
    xf = xn.reshape(-1, D_MODEL)
    n = xf.shape[0]
    logits = (xf @ w_router + b_router).astype(jnp.float32)
    top_val, top_idx = lax.top_k(logits, TOP_K)
    gates = jax.nn.softmax(top_val, axis=-1)
    n_asg = n * TOP_K
    e_flat = top_idx.reshape(-1)
    tok_flat = jnp.arange(n_asg, dtype=jnp.int32) // TOP_K
    order = jnp.argsort(e_flat)
    e_s, tok_s, w_s = e_flat[order], tok_flat[order], gates.reshape(-1)[order]
    counts = jnp.bincount(e_flat, length=N_EXPERTS)
    start = jnp.cumsum(counts) - counts
    pcounts = (counts + MOE_BLOCK - 1) // MOE_BLOCK * MOE_BLOCK
    pend = jnp.cumsum(pcounts)
    pstart = pend - pcounts
    dest = pstart[e_s] + jnp.arange(n_asg) - start[e_s]
    n_rows = -(-n_asg // MOE_BLOCK) * MOE_BLOCK + N_EXPERTS * MOE_BLOCK
    n_blocks = n_rows // MOE_BLOCK
    row_tok = jnp.full((n_rows,), n, jnp.int32).at[dest].set(tok_s)
    row_w = jnp.zeros((n_rows,), jnp.float32).at[dest].set(w_s)
    block_e = jnp.minimum(jnp.searchsorted(pend, jnp.arange(n_blocks) * MOE_BLOCK, side="right"), N_EXPERTS - 1)
    x_pad = jnp.concatenate([xf, jnp.zeros((1, D_MODEL), xf.dtype)], axis=0)
    rows = x_pad[row_tok].reshape(n_blocks, MOE_BLOCK, D_MODEL)

    def expert_block(args):
        xb, e = args
        hdn = xb @ w_exp_in[e] + b_exp_in[e]
        gate, up = jnp.split(hdn, 2, axis=-1)
        gate = jnp.minimum(gate, SWIGLU_LIMIT)
        up = jnp.clip(up, -SWIGLU_LIMIT, SWIGLU_LIMIT)
        act = (up + 1.0) * gate * jax.nn.sigmoid(SWIGLU_ALPHA * gate)
        return act @ w_exp_out[e] + b_exp_out[e]

    out = lax.map(expert_block, (rows, block_e)).reshape(n_rows, D_MODEL)
    y = jax.ops.segment_sum(out.astype(jnp.float32) * row_w[:, None], row_tok, num_segments=n + 1)[:n]
    return y.astype(xn.dtype).reshape(*lead, D_MODEL)


def decoder_layer(h, p, s_gla, da_bufs, chunk, norm_mix_g, w_in, w_gla_alpha, b_gla_alpha, gla_norm_g,
                  w_gla_out, qk_norm_q, qk_norm_k, w_da_out, w_o, norm_moe_g, w_router, b_router,
                  w_exp_in, b_exp_in, w_exp_out, b_exp_out, w_ple_in, w_ple_gate):
    y, s_new, bufs_new = token_mixer(rms_norm(h, norm_mix_g), s_gla, da_bufs, chunk, w_in, w_gla_alpha,
                                     b_gla_alpha, gla_norm_g, w_gla_out, qk_norm_q, qk_norm_k, w_da_out, w_o)
    h = h + y
    h = h + moe(rms_norm(h, norm_moe_g), w_router, b_router, w_exp_in, b_exp_in, w_exp_out, b_exp_out)
    h = h + jax.nn.sigmoid(rms_norm(h) @ w_ple_gate) * (p.astype(h.dtype) @ w_ple_in)
    return h, s_new, bufs_new


def setup_inputs(seed: int = 0) -> dict:
    key = jax.random.key(seed)
    keys = jax.random.split(key, 27)

    def nrm(i, shape, scale):
        return jax.random.normal(keys[i], shape, jnp.float32) * scale

    lb = [min(w, PAST_LEN) for w in DA_WINDOWS]
    L = DEPTH
    return {
        "x_prompt": nrm(0, (BATCH, SEQ, D_MODEL), 1.0),
        "x_sample": nrm(1, (DEC_BATCH, DEC_SEQ, D_MODEL), 1.0),
        "cache_da1_kv": nrm(2, (L, DEC_BATCH, lb[0], 2, DA_HEADS, DA_HEAD_DIM), 1.0),
        "cache_da2_kv": nrm(3, (L, DEC_BATCH, lb[1], 2, DA_HEADS, DA_HEAD_DIM), 1.0),
        "cache_da3_kv": nrm(4, (L, DEC_BATCH, lb[2], 2, DA_HEADS, DA_HEAD_DIM), 1.0),
        "state_gla": nrm(5, (L, DEC_BATCH, GLA_HEADS, GLA_DK, GLA_DV), 0.5),
        "p_prompt": nrm(6, (L, BATCH, SEQ, PLE_DIM), 1.0),
        "p_sample": nrm(7, (L, DEC_BATCH, DEC_SEQ, PLE_DIM), 1.0),
        "norm_mix_g": 1.0 + nrm(8, (L, D_MODEL), 0.05),
        "w_in": nrm(9, (L, D_MODEL, D_IN), D_MODEL ** -0.5),
        "w_gla_alpha": nrm(10, (L, GLA_RANK, GLA_QK), GLA_RANK ** -0.5),
        "b_gla_alpha": nrm(11, (L, GLA_QK), 0.1),
        "gla_norm_g": 1.0 + nrm(12, (L, GLA_DV), 0.05),
        "w_gla_out": nrm(13, (L, GLA_V, D_MODEL), GLA_V ** -0.5),
        "qk_norm_q": 1.0 + nrm(14, (L, N_DA_GROUPS, DA_HEAD_DIM), 0.05),
        "qk_norm_k": 1.0 + nrm(15, (L, N_DA_GROUPS, DA_HEAD_DIM), 0.05),
        "w_da_out": nrm(16, (L, DA_WIDTH, D_MODEL), DA_WIDTH ** -0.5),
        "w_o": nrm(17, (L, D_MODEL, D_MODEL), D_MODEL ** -0.5),
        "norm_moe_g": 1.0 + nrm(18, (L, D_MODEL), 0.05),
        "w_router": nrm(19, (L, D_MODEL, N_EXPERTS), D_MODEL ** -0.5),
        "b_router": nrm(20, (L, N_EXPERTS), 0.01),
        "w_exp_in": nrm(21, (L, N_EXPERTS, D_MODEL, 2 * D_FF), D_MODEL ** -0.5),
        "b_exp_in": nrm(22, (L, N_EXPERTS, 2 * D_FF), 0.02),
        "w_exp_out": nrm(23, (L, N_EXPERTS, D_FF, D_MODEL), D_FF ** -0.5),
        "b_exp_out": nrm(24, (L, N_EXPERTS, D_MODEL), 0.02),
        "w_ple_in": nrm(25, (L, PLE_DIM, D_MODEL), PLE_DIM ** -0.5),
        "w_ple_gate": nrm(26, (L, D_MODEL, D_MODEL), D_MODEL ** -0.5),
    }


def reference(x_prompt, x_sample, cache_da1_kv, cache_da2_kv, cache_da3_kv, state_gla, p_prompt, p_sample,
              norm_mix_g, w_in, w_gla_alpha, b_gla_alpha, gla_norm_g, w_gla_out, qk_norm_q, qk_norm_k,
              w_da_out, w_o, norm_moe_g, w_router, b_router, w_exp_in, b_exp_in, w_exp_out, b_exp_out,
              w_ple_in, w_ple_gate):
    hp, hs = x_prompt, x_sample
    gla_p, gla_s = [], []
    da_p = [[], [], []]
    da_s = [[], [], []]
    for l in range(DEPTH):
        lw = (norm_mix_g[l], w_in[l], w_gla_alpha[l], b_gla_alpha[l], gla_norm_g[l], w_gla_out[l],
              qk_norm_q[l], qk_norm_k[l], w_da_out[l], w_o[l], norm_moe_g[l], w_router[l], b_router[l],
              w_exp_in[l], b_exp_in[l], w_exp_out[l], b_exp_out[l], w_ple_in[l], w_ple_gate[l])
        s0 = jnp.zeros((hp.shape[0], GLA_HEADS, GLA_DK, GLA_DV), jnp.float32)
        hp, sp, bp = decoder_layer(hp, p_prompt[l], s0, None, GLA_CHUNK, *lw)
        hs, ss, bs = decoder_layer(hs, p_sample[l], state_gla[l],
                                   (cache_da1_kv[l], cache_da2_kv[l], cache_da3_kv[l]), hs.shape[1], *lw)
        gla_p.append(sp)
        gla_s.append(ss)
        for g in range(N_DA_GROUPS):
            da_p[g].append(bp[g])
            da_s[g].append(bs[g])
    state_gla_prompt = jnp.stack(gla_p, axis=0)
    state_gla_sample = jnp.stack(gla_s, axis=0)
    da1_prompt = jnp.stack(da_p[0], axis=0)
    da1_sample = jnp.stack(da_s[0], axis=0)
    da2_prompt = jnp.stack(da_p[1], axis=0)
    da2_sample = jnp.stack(da_s[1], axis=0)
    da3_prompt = jnp.stack(da_p[2], axis=0)
    da3_sample = jnp.stack(da_s[2], axis=0)
    return (hp, hs, state_gla_prompt, state_gla_sample, da1_prompt, da1_sample, da2_prompt, da2_sample,
            da3_prompt, da3_sample)
```

```python
import functools

import jax
import jax.numpy as jnp
from jax import lax
from jax.experimental import pallas as pl
from jax.experimental.pallas import tpu as pltpu

F32 = jnp.float32
BF16 = jnp.bfloat16
I32 = jnp.int32

D_MODEL = 1024
GLA_HEADS = 4
GLA_DK = 128
GLA_DV = 256
GLA_QK = GLA_HEADS * GLA_DK
GLA_V = GLA_HEADS * GLA_DV
GLA_RANK = 16
GLA_TAU = 16.0
GLA_CHUNK = 32
DA_WINDOWS = (128, 512, 2048)
DA_DILATIONS = (1, 4, 16)
DA_SPAN = 128
N_DA_GROUPS = 3
DA_HEADS = 8
DA_HEAD_DIM = 64
DA_WIDTH = DA_HEADS * DA_HEAD_DIM
N_EXPERTS = 32
TOP_K = 4
D_FF = D_MODEL
SWIGLU_LIMIT = 7.0
SWIGLU_ALPHA = 1.702
PLE_DIM = 256
EPS = 1e-6

LANES = 128
SUBLANES = 8
VMEM_LIMIT = 56 * 1024 * 1024

PROJ_TN = 512
COL_GQ, COL_GK, COL_GV, COL_GR, COL_GA, COL_GB, COL_DA = 0, 1, 2, 4, 6, 8, 10
PROJ_TILES = COL_DA + 3 * N_DA_GROUPS
PROJ_WIDTH = PROJ_TILES * PROJ_TN
NEG_BIG = -1e30

HIGHEST = lax.Precision.HIGHEST


def _dot(a, b):
    return jnp.dot(a, b, preferred_element_type=F32)


def _dot_nt(a, b):
    return lax.dot_general(a, b, (((1,), (1,)), ((), ())), preferred_element_type=F32)


def _dot_tn(a, b):
    return lax.dot_general(a, b, (((0,), (0,)), ((), ())), preferred_element_type=F32)


def _sigmoid(x):
    return 1.0 / (1.0 + jnp.exp(-x))


def _params(sem, vmem=VMEM_LIMIT):
    return pltpu.CompilerParams(dimension_semantics=sem, vmem_limit_bytes=vmem)


def _proj_kernel(x_ref, g_ref, w_ref, wlr_ref, wal_ref, bal_ref, qkg_ref, bd_ref,
                 p_ref, la_ref, xn_scr):
    j = pl.program_id(1)

    @pl.when(j == 0)
    def _():
        x = x_ref[...]
        ms = jnp.mean(x * x, axis=-1, keepdims=True)
        xn = (x * lax.rsqrt(ms + EPS) * g_ref[...]).astype(BF16)
        xn_scr[...] = xn
        glr = _dot(xn, wlr_ref[...])
        z = _dot(glr.astype(BF16), wal_ref[...]) + bal_ref[...]
        log_sig = jnp.minimum(z, 0.0) - jnp.log(1.0 + jnp.exp(-jnp.abs(z)))
        la_ref[...] = log_sig * (1.0 / GLA_TAU)

    acc = _dot(xn_scr[...], w_ref[...])
    is_qk = jnp.logical_and(j >= COL_DA, (j - COL_DA) % 3 != 2)

    @pl.when(is_qk)
    def _():
        sq = acc * acc
        hi = sq.astype(BF16)
        lo = (sq - hi.astype(F32)).astype(BF16)
        ssum = _dot(hi, bd_ref[...]) + _dot(lo, bd_ref[...])
        p_ref[...] = acc * lax.rsqrt(ssum * (1.0 / DA_HEAD_DIM) + EPS) * qkg_ref[...]

    @pl.when(jnp.logical_not(is_qk))
    def _():
        p_ref[...] = acc


def _proj(x2d, prm, tm):
    m = x2d.shape[0]
    grid = (m // tm, PROJ_TILES)
    const = lambda i, j: (0, 0)
    return pl.pallas_call(
        _proj_kernel,
        out_shape=(jax.ShapeDtypeStruct((m, PROJ_WIDTH), F32),
                   jax.ShapeDtypeStruct((m, GLA_QK), F32)),
        grid=grid,
        in_specs=[
            pl.BlockSpec((tm, D_MODEL), lambda i, j: (i, 0)),
            pl.BlockSpec((1, D_MODEL), const),
            pl.BlockSpec((D_MODEL, PROJ_TN), lambda i, j: (0, j)),
            pl.BlockSpec((D_MODEL, LANES), const),
            pl.BlockSpec((LANES, GLA_QK), const),
            pl.BlockSpec((1, GLA_QK), const),
            pl.BlockSpec((1, PROJ_TN), lambda i, j: (0, jnp.maximum(j - COL_DA, 0))),
            pl.BlockSpec((PROJ_TN, PROJ_TN), const),
        ],
        out_specs=(pl.BlockSpec((tm, PROJ_TN), lambda i, j: (i, j)),
                   pl.BlockSpec((tm, GLA_QK), lambda i, j: (i, 0))),
        scratch_shapes=[pltpu.VMEM((tm, D_MODEL), BF16)],
        compiler_params=_params(("arbitrary", "arbitrary")),
        name="proj",
    )(x2d, prm["norm_mix_g"], prm["w_main"], prm["w_lr"], prm["w_alpha"], prm["b_alpha"],
      prm["qk_gain"], prm["head_ones"])


def _gla_kernel(chunk, t_valid, q_ref, k_ref, v_ref, r_ref, la_ref, s0_ref, g_ref, tri_ref,
                ones_ref, o_ref, s_ref, s_scr):
    t = pl.program_id(1)
    tb = q_ref.shape[0]
    n_chunks = tb // chunk
    row_ok = lax.broadcasted_iota(I32, (tb, GLA_DK), 0) < t_valid

    @pl.when(t == 0)
    def _():
        s_scr[...] = s0_ref[0]

    tri = tri_ref[...]
    causal = tri > 0.0
    ones_cv = jnp.ones((chunk, GLA_DV), F32)
    for h in range(GLA_HEADS):
        ks = slice(h * GLA_DK, (h + 1) * GLA_DK)
        vs = slice(h * GLA_DV, (h + 1) * GLA_DV)
        la = la_ref[:, ks]
        if t_valid < tb:
            la = jnp.where(row_ok, la, 0.0)
        cum = jnp.dot(tri, la, preferred_element_type=F32, precision=HIGHEST)
        tot = jnp.dot(ones_ref[...], la, preferred_element_type=F32, precision=HIGHEST)
        q = q_ref[:, ks] * (GLA_DK ** -0.5)
        k = k_ref[:, ks]
        v = v_ref[:, vs].astype(BF16)
        q_dec = (q * jnp.exp(cum)).astype(BF16)
        k_inv = (k * jnp.exp(-cum)).astype(BF16)
        k_end = (k * jnp.exp(tot - cum)).astype(BF16)
        att = jnp.where(causal, _dot_nt(q_dec, k_inv), 0.0)
        o_intra = _dot(att.astype(BF16), v)
        s = s_scr[h]
        for c in range(n_chunks):
            rows = slice(c * chunk, (c + 1) * chunk)
            o_c = o_intra[rows] + _dot(q_dec[rows], s.astype(BF16))
            u_c = _dot_tn(k_end[rows], v[rows])
            dec = lax.dot_general(la[rows], ones_cv, (((0,), (0,)), ((), ())),
                                  preferred_element_type=F32, precision=HIGHEST)
            s = jnp.exp(dec) * s + u_c
            ms = jnp.mean(o_c * o_c, axis=-1, keepdims=True)
            gate = r_ref[rows, vs]
            o_ref[rows, vs] = o_c * lax.rsqrt(ms + EPS) * g_ref[...] * (gate * _sigmoid(gate))
        s_scr[h] = s

    @pl.when(t == pl.num_programs(1) - 1)
    def _():
        s_ref[0] = s_scr[...]


def _gla(p, la, s0, gla_norm_g, n_batch, t_len, tb, chunk, t_valid=None):
    nt = t_len // tb
    t_valid = tb if t_valid is None else t_valid
    r = jnp.arange(tb)
    same = (r[:, None] // chunk) == (r[None, :] // chunk)
    tri = (same & (r[:, None] >= r[None, :])).astype(F32)
    ones = same.astype(F32)
    row = lambda b, t: b * nt + t
    const = lambda b, t: (0, 0)
    return pl.pallas_call(
        functools.partial(_gla_kernel, chunk, t_valid),
        out_shape=(jax.ShapeDtypeStruct((n_batch * t_len, GLA_V), F32),
                   jax.ShapeDtypeStruct((n_batch, GLA_HEADS, GLA_DK, GLA_DV), F32)),
        grid=(n_batch, nt),
        in_specs=[
            pl.BlockSpec((tb, GLA_QK), lambda b, t: (row(b, t), COL_GQ)),
            pl.BlockSpec((tb, GLA_QK), lambda b, t: (row(b, t), COL_GK)),
            pl.BlockSpec((tb, GLA_V), lambda b, t: (row(b, t), COL_GV * PROJ_TN // GLA_V)),
            pl.BlockSpec((tb, GLA_V), lambda b, t: (row(b, t), COL_GR * PROJ_TN // GLA_V)),
            pl.BlockSpec((tb, GLA_QK), lambda b, t: (row(b, t), 0)),
            pl.BlockSpec((1, GLA_HEADS, GLA_DK, GLA_DV), lambda b, t: (b, 0, 0, 0)),
            pl.BlockSpec((1, GLA_DV), const),
            pl.BlockSpec((tb, tb), const),
            pl.BlockSpec((tb, tb), const),
        ],
        out_specs=(pl.BlockSpec((tb, GLA_V), lambda b, t: (row(b, t), 0)),
                   pl.BlockSpec((1, GLA_HEADS, GLA_DK, GLA_DV), lambda b, t: (b, 0, 0, 0))),
        scratch_shapes=[pltpu.VMEM((GLA_HEADS, GLA_DK, GLA_DV), F32)],
        compiler_params=_params(("arbitrary", "arbitrary")),
        name="gla",
    )(p, p, p, p, la, s0, gla_norm_g, tri, ones)


def _prep_params(norm_mix_g, w_in, w_gla_alpha, b_gla_alpha, gla_norm_g, w_gla_out, qk_norm_q,
                 qk_norm_k, w_da_out, w_o, norm_moe_g, w_router, b_router, b_exp_in, b_exp_out,
                 w_ple_in, w_ple_gate):
    o_lr = GLA_QK * 2 + GLA_V * 2
    o_da = o_lr + GLA_RANK
    o_ga = o_da + 3 * N_DA_GROUPS * DA_WIDTH
    w_main = jnp.concatenate([w_in[:, :o_lr], w_in[:, o_ga:], w_in[:, o_da:o_ga]], axis=1).astype(BF16)
    w_lr = jnp.pad(w_in[:, o_lr:o_da], ((0, 0), (0, LANES - GLA_RANK))).astype(BF16)
    w_alpha = jnp.pad(w_gla_alpha, ((0, LANES - GLA_RANK), (0, 0))).astype(BF16)
    gains = []
    for g in range(N_DA_GROUPS):
        gains.append(jnp.tile(qk_norm_q[g], DA_HEADS) * (DA_HEAD_DIM ** -0.5))
        gains.append(jnp.tile(qk_norm_k[g], DA_HEADS))
        gains.append(jnp.ones((DA_WIDTH,), F32))
    qk_gain = jnp.concatenate(gains)[None, :]
    c = jnp.arange(PROJ_TN)
    head_ones = (c[:, None] // DA_HEAD_DIM == c[None, :] // DA_HEAD_DIM).astype(BF16)
    pad_e = LANES - N_EXPERTS
    return dict(
        norm_mix_g=norm_mix_g[None, :], w_main=w_main, w_lr=w_lr, w_alpha=w_alpha,
        b_alpha=b_gla_alpha[None, :], qk_gain=qk_gain, head_ones=head_ones,
        gla_norm_g=gla_norm_g[None, :], w_gla_out=w_gla_out.astype(BF16),
        w_da_out=w_da_out.astype(BF16), w_o=w_o.astype(BF16), norm_moe_g=norm_moe_g[None, :],
        w_router=jnp.pad(w_router, ((0, 0), (0, pad_e))),
        b_router=jnp.pad(b_router, (0, pad_e), constant_values=NEG_BIG)[None, :],
        b_exp_in=b_exp_in[:, None, :], b_exp_out=b_exp_out[:, None, :],
        w_ple_in=w_ple_in.astype(BF16), w_ple_gate=w_ple_gate.astype(BF16),
    )


def _da_prompt_kernel(q_ref, kc_ref, kp_ref, vc_ref, vp_ref, bias_ref, o_ref, lse_ref):
    bias = bias_ref[0]
    low = lax.broadcasted_iota(I32, (DA_SPAN, LANES), 1) < DA_HEAD_DIM
    for a in range(DA_WIDTH // LANES):
        cs = slice(a * LANES, (a + 1) * LANES)
        q2 = q_ref[0, :, cs]
        k2 = jnp.concatenate([kp_ref[0, :, cs], kc_ref[0, :, cs]], axis=0).astype(BF16)
        v2 = jnp.concatenate([vp_ref[0, :, cs], vc_ref[0, :, cs]], axis=0).astype(BF16)
        res = []
        for sel in (low, jnp.logical_not(low)):
            s = _dot_nt(jnp.where(sel, q2, 0.0).astype(BF16), k2) + bias
            m = jnp.max(s, axis=-1, keepdims=True)
            p = jnp.exp(s - m)
            l = jnp.sum(p, axis=-1, keepdims=True)
            res.append((_dot(p.astype(BF16), v2) / l, m + jnp.log(l)))
        o_ref[0, :, cs] = jnp.where(low, res[0][0], res[1][0])
        lse_ref[0, :, cs] = jnp.where(low, res[0][1], res[1][1])


def _da_bias():
    qi = jnp.arange(DA_SPAN)[:, None]
    kj = jnp.arange(2 * DA_SPAN)[None, :]
    d = kj - qi
    band = (d >= 0) & (d <= DA_SPAN)
    first = band & (kj >= DA_SPAN)
    return jnp.where(jnp.stack([first, band]), 0.0, NEG_BIG).astype(F32)


def _da_prompt(p, g, n_batch, seq):
    dil = DA_DILATIONS[g]
    sub_len = seq // dil
    nb = sub_len // DA_SPAN
    p3 = p.reshape(n_batch, sub_len, dil * PROJ_WIDTH)
    cq, ck, cv = (COL_DA + 3 * g + i for i in range(3))
    blk = (1, DA_SPAN, DA_WIDTH)

    def cur(c):
        return pl.BlockSpec(blk, lambda b, r, n: (b, n, r * PROJ_TILES + c))

    def prev(c):
        return pl.BlockSpec(blk, lambda b, r, n: (b, jnp.maximum(n - 1, 0), r * PROJ_TILES + c))

    out_spec = pl.BlockSpec(blk, lambda b, r, n: (b, n, r))
    out_sds = jax.ShapeDtypeStruct((n_batch, sub_len, dil * DA_WIDTH), F32)
    o, lse = pl.pallas_call(
        _da_prompt_kernel,
        out_shape=(out_sds, out_sds),
        grid=(n_batch, dil, nb),
        in_specs=[cur(cq), cur(ck), prev(ck), cur(cv), prev(cv),
                  pl.BlockSpec((1, DA_SPAN, 2 * DA_SPAN), lambda b, r, n: (jnp.minimum(n, 1), 0, 0))],
        out_specs=(out_spec, out_spec),
        compiler_params=_params(("arbitrary", "arbitrary", "arbitrary")),
        name=f"da_prompt{g}",
    )(p3, p3, p3, p3, p3, _da_bias())
    return o.reshape(n_batch * seq, DA_WIDTH), lse.reshape(n_batch * seq, DA_WIDTH)


def _mix_kernel(n_groups, *refs):
    og_ref = refs[0]
    n_lse = n_groups if n_groups > 1 else 0
    o_refs = refs[1:1 + n_groups]
    l_refs = refs[1 + n_groups:1 + n_groups + n_lse]
    (ga_ref, gb_ref, h_ref, wga_ref, wda_ref, wo_ref, gm_ref, wr_ref, br_ref,
     h1_ref, xn_ref, idx_ref, gate_ref) = refs[1 + n_groups + n_lse:]
    if n_groups == 1:
        o_da = o_refs[0][...]
    else:
        lses = [r[...] for r in l_refs]
        m = functools.reduce(jnp.maximum, lses)
        es = [jnp.exp(l - m) for l in lses]
        num = functools.reduce(jnp.add, [e * r[...] for e, r in zip(es, o_refs)])
        o_da = num / functools.reduce(jnp.add, es)
    y_a = _dot(og_ref[...].astype(BF16), wga_ref[...])
    y_b = _dot(o_da.astype(BF16), wda_ref[...])
    y = _sigmoid(ga_ref[...]) * y_a + _sigmoid(gb_ref[...]) * y_b
    h1 = h_ref[...] + _dot(y.astype(BF16), wo_ref[...])
    h1_ref[...] = h1
    ms = jnp.mean(h1 * h1, axis=-1, keepdims=True)
    xn = h1 * lax.rsqrt(ms + EPS) * gm_ref[...]
    xn_ref[...] = xn
    logits = jnp.dot(xn, wr_ref[...], preferred_element_type=F32, precision=HIGHEST) + br_ref[...]
    lane = lax.broadcasted_iota(I32, logits.shape, 1)
    lane_f = lane.astype(F32)
    work = logits
    idx_out = jnp.zeros(logits.shape, I32)
    val_out = jnp.zeros(logits.shape, F32)
    vals = []
    for k in range(TOP_K):
        top = jnp.max(work, axis=-1, keepdims=True)
        pick = jnp.min(jnp.where(work == top, lane_f, float(LANES)), axis=-1, keepdims=True)
        pick = pick.astype(I32)
        vals.append(top)
        idx_out = jnp.where(lane == k, pick, idx_out)
        work = jnp.where(lane == pick, -3e38, work)
    es = [jnp.exp(v - vals[0]) for v in vals]
    den = functools.reduce(jnp.add, es)
    for k in range(TOP_K):
        val_out = jnp.where(lane == k, es[k] / den, val_out)
    idx_ref[...] = idx_out
    gate_ref[...] = val_out


def _mix(p, o_gla, o_das, lses, h2d, prm, tm):
    m = h2d.shape[0]
    n_groups = len(o_das)
    row = lambda i: (i, 0)
    const = lambda i: (0, 0)
    wide = pl.BlockSpec((tm, D_MODEL), row)
    half = pl.BlockSpec((tm, DA_WIDTH), row)
    small = pl.BlockSpec((tm, LANES), row)
    return pl.pallas_call(
        functools.partial(_mix_kernel, n_groups),
        out_shape=(jax.ShapeDtypeStruct((m, D_MODEL), F32), jax.ShapeDtypeStruct((m, D_MODEL), F32),
                   jax.ShapeDtypeStruct((m, LANES), I32), jax.ShapeDtypeStruct((m, LANES), F32)),
        grid=(m // tm,),
        in_specs=[wide] + [half] * (n_groups + len(lses)) + [
            pl.BlockSpec((tm, D_MODEL), lambda i: (i, COL_GA * PROJ_TN // D_MODEL)),
            pl.BlockSpec((tm, D_MODEL), lambda i: (i, COL_GB * PROJ_TN // D_MODEL)),
            wide,
            pl.BlockSpec((GLA_V, D_MODEL), const),
            pl.BlockSpec((DA_WIDTH, D_MODEL), const),
            pl.BlockSpec((D_MODEL, D_MODEL), const),
            pl.BlockSpec((1, D_MODEL), const),
            pl.BlockSpec((D_MODEL, LANES), const),
            pl.BlockSpec((1, LANES), const),
        ],
        out_specs=(wide, wide, small, small),
        compiler_params=_params(("arbitrary",)),
        name="mix",
    )(o_gla, *o_das, *lses, p, p, h2d, prm["w_gla_out"], prm["w_da_out"], prm["w_o"],
      prm["norm_moe_g"], prm["w_router"], prm["b_router"])


def _da_sample_kernel(t_new, *refs):
    qkv = [refs[3 * g:3 * g + 3] for g in range(N_DA_GROUPS)]
    c_refs = refs[9:12]
    e_ref, et_ref, o_ref = refs[12:15]
    e = e_ref[...]
    et = et_ref[...]
    et_f32 = et.astype(F32)
    row_c = lax.broadcasted_iota(I32, (DA_SPAN, LANES), 0)
    row_n = lax.broadcasted_iota(I32, (SUBLANES, LANES), 0)
    row_o = lax.broadcasted_iota(I32, (SUBLANES, DA_WIDTH), 0)
    out = jnp.zeros((SUBLANES, DA_WIDTH), F32)
    for t in range(t_new):
        lses, accs, dens = [], [], []
        for g in range(N_DA_GROUPS):
            q_ref, k_ref, v_ref = qkv[g]
            qt = q_ref[t:t + 1, :]
            k_new = k_ref[...]
            v_new = v_ref[...]
            if g == 0:
                kc = c_refs[0][0, :, :DA_WIDTH]
                vc = c_refs[0][0, :, DA_WIDTH:]
                new_ok = row_n <= t
            else:
                base = t * 2 * DA_WIDTH
                kc = c_refs[g][0, :, base:base + DA_WIDTH]
                vc = c_refs[g][0, :, base + DA_WIDTH:base + 2 * DA_WIDTH]
                new_ok = row_n == t
            sc = _dot((kc * qt).astype(BF16), e)
            if g == 0:
                sc = jnp.where(row_c >= t, sc, NEG_BIG)
            sn = jnp.where(new_ok, _dot((k_new * qt).astype(BF16), e), NEG_BIG)
            m = jnp.maximum(jnp.max(sc, axis=0, keepdims=True), jnp.max(sn, axis=0, keepdims=True))
            pc = jnp.exp(sc - m)
            pn = jnp.exp(sn - m)
            den = jnp.sum(pc, axis=0, keepdims=True) + jnp.sum(pn, axis=0, keepdims=True)
            acc = (jnp.sum(_dot(pc.astype(BF16), et) * vc, axis=0, keepdims=True)
                   + jnp.sum(_dot(pn.astype(BF16), et) * v_new, axis=0, keepdims=True))
            lses.append(m + jnp.log(den))
            accs.append(acc)
            dens.append(den)
        top = functools.reduce(jnp.maximum, lses)
        ws = [jnp.exp(l - top) for l in lses]
        wsum = functools.reduce(jnp.add, ws)
        o_t = jnp.zeros((1, DA_WIDTH), F32)
        for g in range(N_DA_GROUPS):
            coef = jnp.broadcast_to(ws[g] / (wsum * dens[g]), (SUBLANES, LANES))
            coef = jnp.dot(coef, et_f32, preferred_element_type=F32, precision=HIGHEST)
            o_t = o_t + coef[0:1, :] * accs[g]
        out = jnp.where(row_o == t, o_t, out)
    o_ref[...] = out


def _da_sample(p_s, caches, n_batch, t_pad, t_new):
    c = jnp.arange(DA_WIDTH)
    e = (c[:, None] // DA_HEAD_DIM == jnp.arange(LANES)[None, :]).astype(BF16)
    views = [caches[0].reshape(n_batch, DA_SPAN, 2 * DA_WIDTH)]
    c_specs = [pl.BlockSpec((1, DA_SPAN, 2 * DA_WIDTH), lambda b: (b, 0, 0))]
    for g in (1, 2):
        dil = DA_DILATIONS[g]
        views.append(caches[g].reshape(n_batch, DA_SPAN, dil * 2 * DA_WIDTH))
        c_specs.append(pl.BlockSpec((1, DA_SPAN, t_new * 2 * DA_WIDTH), lambda b: (b, 0, 0)))
    qkv_specs = [pl.BlockSpec((t_pad, DA_WIDTH), lambda b, col=COL_DA + i: (b, col))
                 for i in range(3 * N_DA_GROUPS)]
    return pl.pallas_call(
        functools.partial(_da_sample_kernel, t_new),
        out_shape=jax.ShapeDtypeStruct((n_batch * t_pad, DA_WIDTH), F32),
        grid=(n_batch,),
        in_specs=qkv_specs + c_specs + [pl.BlockSpec((DA_WIDTH, LANES), lambda b: (0, 0)),
                                        pl.BlockSpec((LANES, DA_WIDTH), lambda b: (0, 0))],
        out_specs=pl.BlockSpec((t_pad, DA_WIDTH), lambda b: (b, 0)),
        compiler_params=_params(("arbitrary",)),
        name="da_sample",
    )(*([p_s] * (3 * N_DA_GROUPS)), *views, e, e.T)


def _route_kernel(tm_e, idx_ref, ltri_ref, utri_ref, dest_ref, be_ref, nused_ref,
                  rank_scr, carry_scr, pstart_scr):
    ph = pl.program_id(0)
    i = pl.program_id(1)
    tm = idx_ref.shape[0]
    lane = lax.broadcasted_iota(I32, (tm, LANES), 1)
    idx = idx_ref[...]
    hots = [lane == idx[:, k:k + 1] for k in range(TOP_K)]

    @pl.when(ph == 0)
    def _():
        @pl.when(i == 0)
        def _():
            carry_scr[...] = jnp.zeros_like(carry_scr)

        cnt = functools.reduce(jnp.add, [h.astype(F32) for h in hots])
        before = _dot(ltri_ref[...], cnt.astype(BF16)) + carry_scr[0:1, :]
        rank = jnp.zeros((tm, LANES), F32)
        for k in range(TOP_K):
            r_k = jnp.sum(jnp.where(hots[k], before, 0.0), axis=-1, keepdims=True)
            rank = jnp.where(lane == k, r_k, rank)
        rank_scr[i] = rank
        carry_scr[...] = carry_scr[...] + jnp.sum(cnt, axis=0, keepdims=True)

    @pl.when(ph == 1)
    def _():
        @pl.when(i == 0)
        def _():
            counts = carry_scr[...]
            padded = jnp.floor((counts + (tm_e - 1)) * (1.0 / tm_e)) * tm_e
            pend = jnp.dot(padded, utri_ref[...], preferred_element_type=F32, precision=HIGHEST)
            pstart_scr[...] = pend - padded
            nblk = be_ref.shape[0]
            start = lax.broadcasted_iota(I32, (nblk, LANES), 0).astype(F32) * tm_e
            is_e = lax.broadcasted_iota(I32, (nblk, LANES), 1) < N_EXPERTS
            done = jnp.where(jnp.logical_and(pend[0:1, :] <= start, is_e), 1.0, 0.0)
            be = jnp.minimum(jnp.sum(done, axis=-1, keepdims=True), N_EXPERTS - 1.0)
            be_ref[...] = jnp.broadcast_to(be, (nblk, LANES)).astype(I32)
            total = pend[:, N_EXPERTS - 1:N_EXPERTS] * (1.0 / tm_e)
            nused_ref[...] = jnp.broadcast_to(total, nused_ref.shape).astype(I32)

        dest = rank_scr[i]
        for k in range(TOP_K):
            d_k = jnp.sum(jnp.where(hots[k], pstart_scr[0:1, :], 0.0), axis=-1, keepdims=True)
            dest = dest + jnp.where(lane == k, d_k, 0.0)
        dest_ref[...] = dest.astype(I32)


def _route(idx_all, tm, tm_e, nblk):
    n = idx_all.shape[0]
    nt = n // tm
    r = jnp.arange(tm)
    ltri = (r[:, None] > r[None, :]).astype(BF16)
    c = jnp.arange(LANES)
    utri = ((c[:, None] <= c[None, :]) & (c[:, None] < N_EXPERTS)).astype(F32)
    nblk_pad = -(-nblk // SUBLANES) * SUBLANES
    return pl.pallas_call(
        functools.partial(_route_kernel, tm_e),
        out_shape=(jax.ShapeDtypeStruct((n, LANES), I32),
                   jax.ShapeDtypeStruct((nblk_pad, LANES), I32),
                   jax.ShapeDtypeStruct((SUBLANES, LANES), I32)),
        grid=(2, nt),
        in_specs=[pl.BlockSpec((tm, LANES), lambda ph, i: (i, 0)),
                  pl.BlockSpec((tm, tm), lambda ph, i: (0, 0)),
                  pl.BlockSpec((LANES, LANES), lambda ph, i: (0, 0))],
        out_specs=(pl.BlockSpec((tm, LANES), lambda ph, i: (i * ph, 0)),
                   pl.BlockSpec((nblk_pad, LANES), lambda ph, i: (0, 0)),
                   pl.BlockSpec((SUBLANES, LANES), lambda ph, i: (0, 0))),
        scratch_shapes=[pltpu.VMEM((nt, tm, LANES), F32), pltpu.VMEM((SUBLANES, LANES), F32),
                        pltpu.VMEM((SUBLANES, LANES), F32)],
        compiler_params=_params(("arbitrary", "arbitrary")),
        name="route",
    )(idx_all, ltri, utri)


def _row_copy_loops(tm, make_copy):
    def start(t, carry):
        for k in range(TOP_K):
            make_copy(t, k).start()
        return carry

    def wait(t, carry):
        for k in range(TOP_K):
            make_copy(t, k).wait()
        return carry

    lax.fori_loop(0, tm, start, 0)
    lax.fori_loop(0, tm, wait, 0)


def _dispatch_kernel(dest_ref, x_hbm, xs_in_hbm, xs_hbm, sem):
    del xs_in_hbm
    tm = dest_ref.shape[2] // TOP_K
    base = pl.program_id(0) * tm

    def make_copy(t, k):
        d = dest_ref[0, 0, t * TOP_K + k]
        return pltpu.make_async_copy(x_hbm.at[pl.ds(base + t, 1)], xs_hbm.at[pl.ds(d, 1)], sem)

    _row_copy_loops(tm, make_copy)


def _dispatch(dest3, x2d, xs):
    nt, _, width = dest3.shape
    return pl.pallas_call(
        _dispatch_kernel,
        out_shape=jax.ShapeDtypeStruct(xs.shape, xs.dtype),
        grid=(nt,),
        in_specs=[pl.BlockSpec((1, 1, width), lambda i: (i, 0, 0), memory_space=pltpu.SMEM),
                  pl.BlockSpec(memory_space=pl.ANY),
                  pl.BlockSpec(memory_space=pl.ANY)],
        out_specs=pl.BlockSpec(memory_space=pl.ANY),
        scratch_shapes=[pltpu.SemaphoreType.DMA(())],
        input_output_aliases={2: 0},
        compiler_params=_params(("arbitrary",)),
        name="dispatch",
    )(dest3, x2d, xs)


def _experts_kernel(be_ref, nused_ref, x_ref, wi_ref, bi_ref, wo_ref, bo_ref, o_ref,
                    wi_scr, wo_scr):
    i = pl.program_id(0)
    active = i < nused_ref[0]
    fresh = jnp.logical_or(i == 0, be_ref[i] != be_ref[jnp.maximum(i - 1, 0)])

    @pl.when(jnp.logical_and(active, fresh))
    def _():
        wi_scr[...] = wi_ref[0].astype(BF16)
        wo_scr[...] = wo_ref[0].astype(BF16)

    @pl.when(active)
    def _():
        hdn = _dot(x_ref[...].astype(BF16), wi_scr[...]) + bi_ref[0]
        gate = jnp.minimum(hdn[:, :D_FF], SWIGLU_LIMIT)
        up = jnp.clip(hdn[:, D_FF:], -SWIGLU_LIMIT, SWIGLU_LIMIT)
        act = (up + 1.0) * gate * _sigmoid(SWIGLU_ALPHA * gate)
        o_ref[...] = _dot(act.astype(BF16), wo_scr[...]) + bo_ref[0]

    @pl.when(jnp.logical_not(active))
    def _():
        o_ref[...] = jnp.zeros_like(o_ref)


def _experts(block_e, nused, xs, w_exp_in, b_exp_in, w_exp_out, b_exp_out, tm_e):
    n_rows = xs.shape[0]
    grid_spec = pltpu.PrefetchScalarGridSpec(
        num_scalar_prefetch=2,
        grid=(n_rows // tm_e,),
        in_specs=[
            pl.BlockSpec((tm_e, D_MODEL), lambda i, be, nu: (i, 0)),
            pl.BlockSpec((1, D_MODEL, 2 * D_FF), lambda i, be, nu: (be[i], 0, 0)),
            pl.BlockSpec((1, 1, 2 * D_FF), lambda i, be, nu: (be[i], 0, 0)),
            pl.BlockSpec((1, D_FF, D_MODEL), lambda i, be, nu: (be[i], 0, 0)),
            pl.BlockSpec((1, 1, D_MODEL), lambda i, be, nu: (be[i], 0, 0)),
        ],
        out_specs=pl.BlockSpec((tm_e, D_MODEL), lambda i, be, nu: (i, 0)),
        scratch_shapes=[pltpu.VMEM((D_MODEL, 2 * D_FF), BF16), pltpu.VMEM((D_FF, D_MODEL), BF16)],
    )
    return pl.pallas_call(
        _experts_kernel,
        out_shape=jax.ShapeDtypeStruct((n_rows, D_MODEL), F32),
        grid_spec=grid_spec,
        compiler_params=_params(("arbitrary",)),
        name="experts",
    )(block_e, nused, xs, w_exp_in, b_exp_in, w_exp_out, b_exp_out)


def _final_kernel(dest_ref, h1_ref, gate_ref, p_ref, wpi_ref, wpg_ref, ys_hbm, o_ref,
                  rows_scr, sem):
    tm = h1_ref.shape[0]

    def make_copy(t, k):
        d = dest_ref[0, 0, t * TOP_K + k]
        return pltpu.make_async_copy(ys_hbm.at[pl.ds(d, 1)], rows_scr.at[k, pl.ds(t, 1)], sem)

    _row_copy_loops(tm, make_copy)
    gates = gate_ref[...]
    y = functools.reduce(jnp.add, [gates[:, k:k + 1] * rows_scr[k] for k in range(TOP_K)])
    h2 = h1_ref[...] + y
    ms = jnp.mean(h2 * h2, axis=-1, keepdims=True)
    hn = (h2 * lax.rsqrt(ms + EPS)).astype(BF16)
    gate = _sigmoid(_dot(hn, wpg_ref[...]))
    o_ref[...] = h2 + gate * _dot(p_ref[...].astype(BF16), wpi_ref[...])


def _final(dest3, h1, gates, p2d, ys, prm):
    nt, _, width = dest3.shape
    tm = width // TOP_K
    row = lambda i: (i, 0)
    const = lambda i: (0, 0)
    return pl.pallas_call(
        _final_kernel,
        out_shape=jax.ShapeDtypeStruct(h1.shape, F32),
        grid=(nt,),
        in_specs=[pl.BlockSpec((1, 1, width), lambda i: (i, 0, 0), memory_space=pltpu.SMEM),
                  pl.BlockSpec((tm, D_MODEL), row),
                  pl.BlockSpec((tm, LANES), row),
                  pl.BlockSpec((tm, PLE_DIM), row),
                  pl.BlockSpec((PLE_DIM, D_MODEL), const),
                  pl.BlockSpec((D_MODEL, D_MODEL), const),
                  pl.BlockSpec(memory_space=pl.ANY)],
        out_specs=pl.BlockSpec((tm, D_MODEL), row),
        scratch_shapes=[pltpu.VMEM((TOP_K, tm, D_MODEL), F32), pltpu.SemaphoreType.DMA(())],
        compiler_params=_params(("arbitrary",)),
        name="final",
    )(dest3, h1, gates, p2d, prm["w_ple_in"], prm["w_ple_gate"], ys)


TOKEN_TILE = 256
PROJ_ROW_TILE = 1024
GLA_BLOCK = 256
EXPERT_TILE = 256
SAMPLE_PAD = SUBLANES


def kernel(x_prompt, x_sample, cache_da1_kv, cache_da2_kv, cache_da3_kv, state_gla, p_prompt, p_sample, norm_mix_g, w_in, w_gla_alpha, b_gla_alpha, gla_norm_g, w_gla_out, qk_norm_q, qk_norm_k, w_da_out, w_o, norm_moe_g, w_router, b_router, w_exp_in, b_exp_in, w_exp_out, b_exp_out, w_ple_in, w_ple_gate):
    depth = w_in.shape[0]
    assert depth == 1, "single-layer trunk"
    l = 0
    prm = _prep_params(norm_mix_g[l], w_in[l], w_gla_alpha[l], b_gla_alpha[l], gla_norm_g[l],
                       w_gla_out[l], qk_norm_q[l], qk_norm_k[l], w_da_out[l], w_o[l], norm_moe_g[l],
                       w_router[l], b_router[l], b_exp_in[l], b_exp_out[l], w_ple_in[l], w_ple_gate[l])
    n_b, seq, _ = x_prompt.shape
    d_b, t_new, _ = x_sample.shape
    caches = (cache_da1_kv[l], cache_da2_kv[l], cache_da3_kv[l])
    assert seq % (DA_SPAN * DA_DILATIONS[-1]) == 0 and seq % PROJ_ROW_TILE == 0
    assert t_new <= SAMPLE_PAD and (d_b * SAMPLE_PAD) % TOKEN_TILE == 0
    for g in range(N_DA_GROUPS):
        assert caches[g].shape[1] == DA_WINDOWS[g]

    hp = x_prompt.reshape(n_b * seq, D_MODEL)
    pp, la_p = _proj(hp, prm, PROJ_ROW_TILE)
    zero_state = jnp.zeros((n_b, GLA_HEADS, GLA_DK, GLA_DV), F32)
    og_p, state_p = _gla(pp, la_p, zero_state, prm["gla_norm_g"], n_b, seq, GLA_BLOCK, GLA_CHUNK)
    da_p = [_da_prompt(pp, g, n_b, seq) for g in range(N_DA_GROUPS)]
    h1_p, xn_p, idx_p, gate_p = _mix(pp, og_p, [o for o, _ in da_p], [s for _, s in da_p], hp, prm,
                                     TOKEN_TILE)

    pad_t = ((0, 0), (0, SAMPLE_PAD - t_new), (0, 0))
    hs = jnp.pad(x_sample, pad_t).reshape(d_b * SAMPLE_PAD, D_MODEL)
    ps, la_s = _proj(hs, prm, TOKEN_TILE)
    og_s, state_s = _gla(ps, la_s, state_gla[l], prm["gla_norm_g"], d_b, SAMPLE_PAD, SAMPLE_PAD,
                         SAMPLE_PAD, t_valid=t_new)
    o_s = _da_sample(ps, caches, d_b, SAMPLE_PAD, t_new)
    h1_s, xn_s, idx_s, gate_s = _mix(ps, og_s, [o_s], [], hs, prm, TOKEN_TILE)

    n_p, n_s = hp.shape[0], hs.shape[0]
    n_rows = (n_p + n_s) * TOP_K + N_EXPERTS * EXPERT_TILE
    dest, block_e, nused = _route(jnp.concatenate([idx_p, idx_s], axis=0), TOKEN_TILE, EXPERT_TILE,
                                  n_rows // EXPERT_TILE)
    dest3 = dest[:, :TOP_K].reshape(-1, 1, TOKEN_TILE * TOP_K)
    dest3_p, dest3_s = dest3[:n_p // TOKEN_TILE], dest3[n_p // TOKEN_TILE:]
    xs = jnp.zeros((n_rows, D_MODEL), F32)
    xs = _dispatch(dest3_p, xn_p, xs)
    xs = _dispatch(dest3_s, xn_s, xs)
    ys = _experts(block_e[:n_rows // EXPERT_TILE, 0], nused[0, :1], xs, w_exp_in[l], prm["b_exp_in"],
                  w_exp_out[l], prm["b_exp_out"], EXPERT_TILE)
    out_p = _final(dest3_p, h1_p, gate_p, p_prompt[l].reshape(n_p, PLE_DIM), ys, prm)
    out_s = _final(dest3_s, h1_s, gate_s, jnp.pad(p_sample[l], pad_t).reshape(n_s, PLE_DIM), ys, prm)

    y_prompt = out_p.reshape(n_b, seq, D_MODEL)
    y_sample = out_s.reshape(d_b, SAMPLE_PAD, D_MODEL)[:, :t_new]
    pp3 = pp.reshape(n_b, seq, PROJ_WIDTH)
    ps3 = ps.reshape(d_b, SAMPLE_PAD, PROJ_WIDTH)
    outs_p, outs_s = [], []
    for g in range(N_DA_GROUPS):
        win = min(DA_WINDOWS[g], seq)
        ck = (COL_DA + 3 * g + 1) * PROJ_TN
        kv_p = pp3[:, seq - win:, ck:ck + 2 * DA_WIDTH]
        outs_p.append(kv_p.reshape(1, n_b, win, 2, DA_HEADS, DA_HEAD_DIM))
        kv_s = ps3[:, :t_new, ck:ck + 2 * DA_WIDTH].reshape(d_b, t_new, 2, DA_HEADS, DA_HEAD_DIM)
        outs_s.append(jnp.concatenate([caches[g][:, t_new:], kv_s], axis=1)[None])
    return (y_prompt, y_sample, state_p[None], state_s[None], outs_p[0], outs_s[0], outs_p[1],
            outs_s[1], outs_p[2], outs_s[2])
```

```python
import functools

import jax
import jax.numpy as jnp
from jax import lax
from jax.experimental import pallas as pl
from jax.experimental.pallas import tpu as pltpu

F32 = jnp.float32
BF16 = jnp.bfloat16
I32 = jnp.int32

D_MODEL = 1024
GLA_HEADS = 4
GLA_DK = 128
GLA_DV = 256
GLA_QK = GLA_HEADS * GLA_DK
GLA_V = GLA_HEADS * GLA_DV
GLA_RANK = 16
GLA_TAU = 16.0
GLA_CHUNK = 32
DA_WINDOWS = (128, 512, 2048)
DA_DILATIONS = (1, 4, 16)
DA_SPAN = 128
N_DA_GROUPS = 3
DA_HEADS = 8
DA_HEAD_DIM = 64
DA_WIDTH = DA_HEADS * DA_HEAD_DIM
N_EXPERTS = 32
TOP_K = 4
D_FF = D_MODEL
SWIGLU_LIMIT = 7.0
SWIGLU_ALPHA = 1.702
PLE_DIM = 256
EPS = 1e-6

LANES = 128
SUBLANES = 8
VMEM_LIMIT = 56 * 1024 * 1024

PROJ_TN = 512
COL_GQ, COL_GK, COL_GV, COL_GR, COL_GA, COL_GB, COL_DA = 0, 1, 2, 4, 6, 8, 10
PROJ_TILES = COL_DA + 3 * N_DA_GROUPS
PROJ_WIDTH = PROJ_TILES * PROJ_TN
NEG_BIG = -1e30

HIGHEST = lax.Precision.HIGHEST


def _dot(a, b):
    return jnp.dot(a, b, preferred_element_type=F32)


def _dot_nt(a, b):
    return lax.dot_general(a, b, (((1,), (1,)), ((), ())), preferred_element_type=F32)


def _dot_tn(a, b):
    return lax.dot_general(a, b, (((0,), (0,)), ((), ())), preferred_element_type=F32)


def _sigmoid(x):
    return 1.0 / (1.0 + jnp.exp(-x))


def _params(sem, vmem=VMEM_LIMIT):
    return pltpu.CompilerParams(dimension_semantics=sem, vmem_limit_bytes=vmem)


def _proj_kernel(dilate, x_ref, g_ref, w_ref, wlr_ref, wal_ref, bal_ref, qkg_ref, bd_ref,
                 p_ref, la_ref, xn_scr, stage_scr):
    j = pl.program_id(1)
    tm = x_ref.shape[0]

    @pl.when(j == 0)
    def _():
        x = x_ref[...]
        ms = jnp.mean(x * x, axis=-1, keepdims=True)
        xn = (x * lax.rsqrt(ms + EPS) * g_ref[...]).astype(BF16)
        xn_scr[...] = xn
        glr = _dot(xn, wlr_ref[...])
        z = _dot(glr.astype(BF16), wal_ref[...]) + bal_ref[...]
        log_sig = jnp.minimum(z, 0.0) - jnp.log(1.0 + jnp.exp(-jnp.abs(z)))
        la_ref[...] = log_sig * (1.0 / GLA_TAU)

    acc = _dot(xn_scr[...], w_ref[...])
    is_qk = jnp.logical_and(j >= COL_DA, (j - COL_DA) % 3 != 2)
    group = (j - COL_DA) // 3
    staged = [jnp.logical_and(j >= COL_DA, group == g) if dilate and DA_DILATIONS[g] > 1 else None
              for g in range(N_DA_GROUPS)]
    any_staged = functools.reduce(jnp.logical_or, [s for s in staged if s is not None], False)

    def qk_normed():
        sq = acc * acc
        hi = sq.astype(BF16)
        lo = (sq - hi.astype(F32)).astype(BF16)
        ssum = _dot(hi, bd_ref[...]) + _dot(lo, bd_ref[...])
        return acc * lax.rsqrt(ssum * (1.0 / DA_HEAD_DIM) + EPS) * qkg_ref[...]

    def emit(value):
        if any_staged is False:
            p_ref[...] = value
            return

        @pl.when(any_staged)
        def _():
            for c in range(PROJ_TN // LANES):
                stage_scr[c] = value[:, c * LANES:(c + 1) * LANES]

        @pl.when(jnp.logical_not(any_staged))
        def _():
            p_ref[...] = value

    @pl.when(is_qk)
    def _():
        emit(qk_normed())

    @pl.when(jnp.logical_not(is_qk))
    def _():
        emit(acc)

    for g in range(N_DA_GROUPS):
        if staged[g] is None:
            continue
        dil = DA_DILATIONS[g]
        per = tm // dil

        @pl.when(staged[g])
        def _(dil=dil, per=per):
            for r in range(dil):
                for c in range(PROJ_TN // LANES):
                    p_ref[r * per:(r + 1) * per, c * LANES:(c + 1) * LANES] = (
                        stage_scr[c, pl.ds(r, per, stride=dil), :])


def _proj(x2d, prm, tm, dilate):
    m = x2d.shape[0]
    grid = (m // tm, PROJ_TILES)
    const = lambda i, j: (0, 0)
    return pl.pallas_call(
        functools.partial(_proj_kernel, dilate),
        out_shape=(jax.ShapeDtypeStruct((m, PROJ_WIDTH), F32),
                   jax.ShapeDtypeStruct((m, GLA_QK), F32)),
        grid=grid,
        in_specs=[
            pl.BlockSpec((tm, D_MODEL), lambda i, j: (i, 0)),
            pl.BlockSpec((1, D_MODEL), const),
            pl.BlockSpec((D_MODEL, PROJ_TN), lambda i, j: (0, j)),
            pl.BlockSpec((D_MODEL, LANES), const),
            pl.BlockSpec((LANES, GLA_QK), const),
            pl.BlockSpec((1, GLA_QK), const),
            pl.BlockSpec((1, PROJ_TN), lambda i, j: (0, jnp.maximum(j - COL_DA, 0))),
            pl.BlockSpec((PROJ_TN, PROJ_TN), const),
        ],
        out_specs=(pl.BlockSpec((tm, PROJ_TN), lambda i, j: (i, j)),
                   pl.BlockSpec((tm, GLA_QK), lambda i, j: (i, 0))),
        scratch_shapes=[pltpu.VMEM((tm, D_MODEL), BF16),
                        pltpu.VMEM((PROJ_TN // LANES, tm, LANES), F32)],
        compiler_params=_params(("arbitrary", "arbitrary")),
        name="proj",
    )(x2d, prm["norm_mix_g"], prm["w_main"], prm["w_lr"], prm["w_alpha"], prm["b_alpha"],
      prm["qk_gain"], prm["head_ones"])


def _gla_kernel(chunk, t_valid, q_ref, k_ref, v_ref, r_ref, la_ref, s0_ref, g_ref, tri_ref,
                ones_ref, o_ref, s_ref, s_scr):
    t = pl.program_id(1)
    tb = q_ref.shape[0]
    n_chunks = tb // chunk
    row_ok = lax.broadcasted_iota(I32, (tb, GLA_DK), 0) < t_valid

    @pl.when(t == 0)
    def _():
        s_scr[...] = s0_ref[0]

    tri = tri_ref[...]
    causal = tri > 0.0
    ones_cv = jnp.ones((chunk, GLA_DV), F32)
    for h in range(GLA_HEADS):
        ks = slice(h * GLA_DK, (h + 1) * GLA_DK)
        vs = slice(h * GLA_DV, (h + 1) * GLA_DV)
        la = la_ref[:, ks]
        if t_valid < tb:
            la = jnp.where(row_ok, la, 0.0)
        cum = jnp.dot(tri, la, preferred_element_type=F32, precision=HIGHEST)
        tot = jnp.dot(ones_ref[...], la, preferred_element_type=F32, precision=HIGHEST)
        q = q_ref[:, ks] * (GLA_DK ** -0.5)
        k = k_ref[:, ks]
        v = v_ref[:, vs].astype(BF16)
        q_dec = (q * jnp.exp(cum)).astype(BF16)
        k_inv = (k * jnp.exp(-cum)).astype(BF16)
        k_end = (k * jnp.exp(tot - cum)).astype(BF16)
        att = jnp.where(causal, _dot_nt(q_dec, k_inv), 0.0)
        o_intra = _dot(att.astype(BF16), v)
        s = s_scr[h]
        for c in range(n_chunks):
            rows = slice(c * chunk, (c + 1) * chunk)
            o_c = o_intra[rows] + _dot(q_dec[rows], s.astype(BF16))
            u_c = _dot_tn(k_end[rows], v[rows])
            dec = lax.dot_general(la[rows], ones_cv, (((0,), (0,)), ((), ())),
                                  preferred_element_type=F32, precision=HIGHEST)
            s = jnp.exp(dec) * s + u_c
            ms = jnp.mean(o_c * o_c, axis=-1, keepdims=True)
            gate = r_ref[rows, vs]
            o_ref[rows, vs] = o_c * lax.rsqrt(ms + EPS) * g_ref[...] * (gate * _sigmoid(gate))
        s_scr[h] = s

    @pl.when(t == pl.num_programs(1) - 1)
    def _():
        s_ref[0] = s_scr[...]


def _gla(p, la, s0, gla_norm_g, n_batch, t_len, tb, chunk, t_valid=None):
    nt = t_len // tb
    t_valid = tb if t_valid is None else t_valid
    r = jnp.arange(tb)
    same = (r[:, None] // chunk) == (r[None, :] // chunk)
    tri = (same & (r[:, None] >= r[None, :])).astype(F32)
    ones = same.astype(F32)
    row = lambda b, t: b * nt + t
    const = lambda b, t: (0, 0)
    return pl.pallas_call(
        functools.partial(_gla_kernel, chunk, t_valid),
        out_shape=(jax.ShapeDtypeStruct((n_batch * t_len, GLA_V), F32),
                   jax.ShapeDtypeStruct((n_batch, GLA_HEADS, GLA_DK, GLA_DV), F32)),
        grid=(n_batch, nt),
        in_specs=[
            pl.BlockSpec((tb, GLA_QK), lambda b, t: (row(b, t), COL_GQ)),
            pl.BlockSpec((tb, GLA_QK), lambda b, t: (row(b, t), COL_GK)),
            pl.BlockSpec((tb, GLA_V), lambda b, t: (row(b, t), COL_GV * PROJ_TN // GLA_V)),
            pl.BlockSpec((tb, GLA_V), lambda b, t: (row(b, t), COL_GR * PROJ_TN // GLA_V)),
            pl.BlockSpec((tb, GLA_QK), lambda b, t: (row(b, t), 0)),
            pl.BlockSpec((1, GLA_HEADS, GLA_DK, GLA_DV), lambda b, t: (b, 0, 0, 0)),
            pl.BlockSpec((1, GLA_DV), const),
            pl.BlockSpec((tb, tb), const),
            pl.BlockSpec((tb, tb), const),
        ],
        out_specs=(pl.BlockSpec((tb, GLA_V), lambda b, t: (row(b, t), 0)),
                   pl.BlockSpec((1, GLA_HEADS, GLA_DK, GLA_DV), lambda b, t: (b, 0, 0, 0))),
        scratch_shapes=[pltpu.VMEM((GLA_HEADS, GLA_DK, GLA_DV), F32)],
        compiler_params=_params(("arbitrary", "arbitrary")),
        name="gla",
    )(p, p, p, p, la, s0, gla_norm_g, tri, ones)


def _prep_params(norm_mix_g, w_in, w_gla_alpha, b_gla_alpha, gla_norm_g, w_gla_out, qk_norm_q,
                 qk_norm_k, w_da_out, w_o, norm_moe_g, w_router, b_router, b_exp_in, b_exp_out,
                 w_ple_in, w_ple_gate):
    o_lr = GLA_QK * 2 + GLA_V * 2
    o_da = o_lr + GLA_RANK
    o_ga = o_da + 3 * N_DA_GROUPS * DA_WIDTH
    w_main = jnp.concatenate([w_in[:, :o_lr], w_in[:, o_ga:], w_in[:, o_da:o_ga]], axis=1).astype(BF16)
    w_lr = jnp.pad(w_in[:, o_lr:o_da], ((0, 0), (0, LANES - GLA_RANK))).astype(BF16)
    w_alpha = jnp.pad(w_gla_alpha, ((0, LANES - GLA_RANK), (0, 0))).astype(BF16)
    gains = []
    for g in range(N_DA_GROUPS):
        gains.append(jnp.tile(qk_norm_q[g], DA_HEADS) * (DA_HEAD_DIM ** -0.5))
        gains.append(jnp.tile(qk_norm_k[g], DA_HEADS))
        gains.append(jnp.ones((DA_WIDTH,), F32))
    qk_gain = jnp.concatenate(gains)[None, :]
    c = jnp.arange(PROJ_TN)
    head_ones = (c[:, None] // DA_HEAD_DIM == c[None, :] // DA_HEAD_DIM).astype(BF16)
    pad_e = LANES - N_EXPERTS
    return dict(
        norm_mix_g=norm_mix_g[None, :], w_main=w_main, w_lr=w_lr, w_alpha=w_alpha,
        b_alpha=b_gla_alpha[None, :], qk_gain=qk_gain, head_ones=head_ones,
        gla_norm_g=gla_norm_g[None, :], w_gla_out=w_gla_out.astype(BF16),
        w_da_out=w_da_out.astype(BF16), w_o=w_o.astype(BF16), norm_moe_g=norm_moe_g[None, :],
        w_router=jnp.pad(w_router, ((0, 0), (0, pad_e))),
        b_router=jnp.pad(b_router, (0, pad_e), constant_values=NEG_BIG)[None, :],
        b_exp_in=b_exp_in[:, None, :], b_exp_out=b_exp_out[:, None, :],
        w_ple_in=w_ple_in.astype(BF16), w_ple_gate=w_ple_gate.astype(BF16),
    )


def _da_prompt_kernel(q_ref, kc_ref, kp_ref, vc_ref, vp_ref, bias_ref, o_ref, lse_ref):
    bias = bias_ref[0]
    low = lax.broadcasted_iota(I32, (DA_SPAN, LANES), 1) < DA_HEAD_DIM
    pieces = q_ref.shape[1:4:2]

    def rows(ref, cs):
        return ref[0, :, 0, :, cs].reshape(DA_SPAN, LANES)

    for a in range(DA_WIDTH // LANES):
        cs = slice(a * LANES, (a + 1) * LANES)
        q2 = rows(q_ref, cs)
        k2 = jnp.concatenate([rows(kp_ref, cs), rows(kc_ref, cs)], axis=0).astype(BF16)
        v2 = jnp.concatenate([rows(vp_ref, cs), rows(vc_ref, cs)], axis=0).astype(BF16)
        res = []
        for sel in (low, jnp.logical_not(low)):
            s = _dot_nt(jnp.where(sel, q2, 0.0).astype(BF16), k2) + bias
            m = jnp.max(s, axis=-1, keepdims=True)
            p = jnp.exp(s - m)
            l = jnp.sum(p, axis=-1, keepdims=True)
            res.append((_dot(p.astype(BF16), v2) / l, m + jnp.log(l)))
        o_ref[0, :, 0, :, cs] = jnp.where(low, res[0][0], res[1][0]).reshape(*pieces, LANES)
        lse_ref[0, :, 0, :, cs] = jnp.where(low, res[0][1], res[1][1]).reshape(*pieces, LANES)


def _da_bias():
    qi = jnp.arange(DA_SPAN)[:, None]
    kj = jnp.arange(2 * DA_SPAN)[None, :]
    d = kj - qi
    band = (d >= 0) & (d <= DA_SPAN)
    first = band & (kj >= DA_SPAN)
    return jnp.where(jnp.stack([first, band]), 0.0, NEG_BIG).astype(F32)


def _da_prompt(p, g, n_batch, seq, tile):
    dil = DA_DILATIONS[g]
    per = tile // dil
    nb = seq // dil // DA_SPAN
    p5 = p.reshape(n_batch, seq // tile, dil, per, PROJ_WIDTH)
    cq, ck, cv = (COL_DA + 3 * g + i for i in range(3))
    if per >= DA_SPAN:
        assert per % DA_SPAN == 0
        blk = (1, 1, 1, DA_SPAN, DA_WIDTH)
        place = lambda n: (n // (per // DA_SPAN), n % (per // DA_SPAN))
    else:
        assert DA_SPAN % per == 0 and per % SUBLANES == 0
        blk = (1, DA_SPAN // per, 1, per, DA_WIDTH)
        place = lambda n: (n, 0)

    def spec(c, back):
        def index(b, r, n):
            tile_i, row_i = place(jnp.maximum(n - back, 0))
            return (b, tile_i, r, row_i, c)
        return pl.BlockSpec(blk, index)

    out_sds = jax.ShapeDtypeStruct((n_batch, seq // tile, dil, per, DA_WIDTH), F32)
    return pl.pallas_call(
        _da_prompt_kernel,
        out_shape=(out_sds, out_sds),
        grid=(n_batch, dil, nb),
        in_specs=[spec(cq, 0), spec(ck, 0), spec(ck, 1), spec(cv, 0), spec(cv, 1),
                  pl.BlockSpec((1, DA_SPAN, 2 * DA_SPAN), lambda b, r, n: (jnp.minimum(n, 1), 0, 0))],
        out_specs=(spec(0, 0), spec(0, 0)),
        compiler_params=_params(("arbitrary", "arbitrary", "arbitrary")),
        name=f"da_prompt{g}",
    )(p5, p5, p5, p5, p5, _da_bias())


def _mix_kernel(dils, *refs):
    og_ref = refs[0]
    n_groups = len(dils) if dils else 1
    n_lse = n_groups if dils else 0
    o_refs = refs[1:1 + n_groups]
    l_refs = refs[1 + n_groups:1 + n_groups + n_lse]
    (ga_ref, gb_ref, h_ref, wga_ref, wda_ref, wo_ref, gm_ref, wr_ref, br_ref,
     h1_ref, xn_ref, idx_ref, gate_ref) = refs[1 + n_groups + n_lse:14 + n_groups + n_lse]
    scratch = iter(refs[14 + n_groups + n_lse:])

    def token_major(ref, dil):
        if dil == 1:
            return ref[0, 0]
        scr = next(scratch)
        per = ref.shape[2]
        for r in range(dil):
            for c in range(DA_WIDTH // LANES):
                scr[c, pl.ds(r, per, stride=dil), :] = ref[0, r, :, c * LANES:(c + 1) * LANES]
        return jnp.concatenate([scr[c] for c in range(DA_WIDTH // LANES)], axis=1)

    if not dils:
        o_da = o_refs[0][...]
    else:
        outs = [token_major(r, d) for r, d in zip(o_refs, dils)]
        lses = [token_major(r, d) for r, d in zip(l_refs, dils)]
        m = functools.reduce(jnp.maximum, lses)
        es = [jnp.exp(l - m) for l in lses]
        num = functools.reduce(jnp.add, [e * o for e, o in zip(es, outs)])
        o_da = num / functools.reduce(jnp.add, es)
    y_a = _dot(og_ref[...].astype(BF16), wga_ref[...])
    y_b = _dot(o_da.astype(BF16), wda_ref[...])
    y = _sigmoid(ga_ref[...]) * y_a + _sigmoid(gb_ref[...]) * y_b
    h1 = h_ref[...] + _dot(y.astype(BF16), wo_ref[...])
    h1_ref[...] = h1
    ms = jnp.mean(h1 * h1, axis=-1, keepdims=True)
    xn = h1 * lax.rsqrt(ms + EPS) * gm_ref[...]
    xn_ref[...] = xn
    logits = jnp.dot(xn, wr_ref[...], preferred_element_type=F32, precision=HIGHEST) + br_ref[...]
    lane = lax.broadcasted_iota(I32, logits.shape, 1)
    lane_f = lane.astype(F32)
    work = logits
    idx_out = jnp.zeros(logits.shape, I32)
    val_out = jnp.zeros(logits.shape, F32)
    vals = []
    for k in range(TOP_K):
        top = jnp.max(work, axis=-1, keepdims=True)
        pick = jnp.min(jnp.where(work == top, lane_f, float(LANES)), axis=-1, keepdims=True)
        pick = pick.astype(I32)
        vals.append(top)
        idx_out = jnp.where(lane == k, pick, idx_out)
        work = jnp.where(lane == pick, -3e38, work)
    es = [jnp.exp(v - vals[0]) for v in vals]
    den = functools.reduce(jnp.add, es)
    for k in range(TOP_K):
        val_out = jnp.where(lane == k, es[k] / den, val_out)
    idx_ref[...] = idx_out
    gate_ref[...] = val_out


def _mix(p, o_gla, o_das, lses, h2d, prm, tm, tile=None):
    m = h2d.shape[0]
    row = lambda i: (i, 0)
    const = lambda i: (0, 0)
    wide = pl.BlockSpec((tm, D_MODEL), row)
    small = pl.BlockSpec((tm, LANES), row)
    if lses:
        dils = DA_DILATIONS
        sub = tile // tm
        da_in, da_specs, scratch = [], [], []
        for arr, dil in list(zip(o_das, dils)) + list(zip(lses, dils)):
            assert (tm // dil) % SUBLANES == 0
            da_in.append(arr.reshape(-1, dil, tile // dil, DA_WIDTH))
            da_specs.append(pl.BlockSpec((1, dil, tm // dil, DA_WIDTH),
                                         lambda i: (i // sub, 0, i % sub, 0)))
            if dil > 1:
                scratch.append(pltpu.VMEM((DA_WIDTH // LANES, tm, LANES), F32))
    else:
        dils, da_in, scratch = None, list(o_das), []
        da_specs = [pl.BlockSpec((tm, DA_WIDTH), row)]
    return pl.pallas_call(
        functools.partial(_mix_kernel, dils),
        out_shape=(jax.ShapeDtypeStruct((m, D_MODEL), F32), jax.ShapeDtypeStruct((m, D_MODEL), F32),
                   jax.ShapeDtypeStruct((m, LANES), I32), jax.ShapeDtypeStruct((m, LANES), F32)),
        grid=(m // tm,),
        scratch_shapes=scratch,
        in_specs=[wide] + da_specs + [
            pl.BlockSpec((tm, D_MODEL), lambda i: (i, COL_GA * PROJ_TN // D_MODEL)),
            pl.BlockSpec((tm, D_MODEL), lambda i: (i, COL_GB * PROJ_TN // D_MODEL)),
            wide,
            pl.BlockSpec((GLA_V, D_MODEL), const),
            pl.BlockSpec((DA_WIDTH, D_MODEL), const),
            pl.BlockSpec((D_MODEL, D_MODEL), const),
            pl.BlockSpec((1, D_MODEL), const),
            pl.BlockSpec((D_MODEL, LANES), const),
            pl.BlockSpec((1, LANES), const),
        ],
        out_specs=(wide, wide, small, small),
        compiler_params=_params(("arbitrary",)),
        name="mix",
    )(o_gla, *da_in, p, p, h2d, prm["w_gla_out"], prm["w_da_out"], prm["w_o"],
      prm["norm_moe_g"], prm["w_router"], prm["b_router"])


def _da_sample_kernel(t_new, *refs):
    qkv = [refs[3 * g:3 * g + 3] for g in range(N_DA_GROUPS)]
    c_refs = refs[9:12]
    sel_ref = refs[12]
    o_ref = refs[13]
    new_refs = refs[14:17]
    t_pad = o_ref.shape[0]
    q_row = lax.broadcasted_iota(I32, (t_pad, t_pad), 0)
    k_row = lax.broadcasted_iota(I32, (t_pad, t_pad), 1)
    lane = lax.broadcasted_iota(I32, (DA_HEAD_DIM, LANES), 1)
    heads = []
    for hh in range(LANES // DA_HEAD_DIM):
        hs = slice(hh * DA_HEAD_DIM, (hh + 1) * DA_HEAD_DIM)
        lses, accs, dens = [], [], []
        for g in range(N_DA_GROUPS):
            dil, win = DA_DILATIONS[g], DA_WINDOWS[g]
            q_ref, k_ref, v_ref = qkv[g]
            q = q_ref[:, hs].astype(BF16)
            k_new = k_ref[:, hs]
            v_new = v_ref[:, hs]
            kt = c_refs[g][0, 0, hh]
            vt = c_refs[g][0, 1, hh]
            t_idx = lax.broadcasted_iota(I32, (t_pad, win), 0)
            off = lax.broadcasted_iota(I32, (t_pad, win), 1) - t_idx
            ok = jnp.logical_and(off >= 0, jnp.bitwise_and(off, dil - 1) == 0)
            s = jnp.where(ok, _dot(q, kt.astype(BF16)), NEG_BIG)
            back = q_row - k_row
            ok_new = jnp.logical_and(jnp.logical_and(back >= 0, jnp.bitwise_and(back, dil - 1) == 0),
                                     k_row < t_new)
            s_new = jnp.where(ok_new, _dot_nt(q, k_new.astype(BF16)), NEG_BIG)
            m = jnp.maximum(jnp.max(s, axis=-1, keepdims=True), jnp.max(s_new, axis=-1, keepdims=True))
            p = jnp.exp(s - m)
            p_new = jnp.exp(s_new - m)
            dens.append(jnp.sum(p, axis=-1, keepdims=True) + jnp.sum(p_new, axis=-1, keepdims=True))
            accs.append(_dot_nt(p.astype(BF16), vt.astype(BF16))
                        + _dot(p_new.astype(BF16), v_new.astype(BF16)))
            lses.append(m + jnp.log(dens[-1]))
            for kv, (old, new) in enumerate(((kt, k_new), (vt, v_new))):
                moved = pltpu.roll(old, win - t_new, axis=1)
                tail = lax.dot_general(new, sel_ref[...], (((0,), (0,)), ((), ())),
                                       preferred_element_type=F32, precision=HIGHEST)
                tail = jnp.where(lane >= LANES - t_new, tail, moved[:, win - LANES:])
                if win > LANES:
                    new_refs[g][0, kv, hh, :, :win - LANES] = moved[:, :win - LANES]
                new_refs[g][0, kv, hh, :, win - LANES:] = tail
        top = functools.reduce(jnp.maximum, lses)
        ws = [jnp.exp(l - top) for l in lses]
        wsum = functools.reduce(jnp.add, ws)
        heads.append(functools.reduce(
            jnp.add, [accs[g] * (ws[g] / (wsum * dens[g])) for g in range(N_DA_GROUPS)]))
    o_ref[...] = jnp.concatenate(heads, axis=1)


def _da_sample(p_s, caches_t, n_batch, t_pad, t_new):
    pair = LANES // DA_HEAD_DIM
    t = jnp.arange(t_pad)[:, None]
    sel = ((jnp.arange(LANES)[None, :] == LANES - t_new + t) & (t < t_new)).astype(F32)
    per_tile = PROJ_TN // LANES
    qkv_specs = [pl.BlockSpec((t_pad, LANES), lambda b, a, col=COL_DA + i: (b, col * per_tile + a))
                 for i in range(3 * N_DA_GROUPS)]
    c_specs = [pl.BlockSpec((1, 2, pair, DA_HEAD_DIM, DA_WINDOWS[g]), lambda b, a: (b, 0, a, 0, 0))
               for g in range(N_DA_GROUPS)]
    return pl.pallas_call(
        functools.partial(_da_sample_kernel, t_new),
        out_shape=(jax.ShapeDtypeStruct((n_batch * t_pad, DA_WIDTH), F32),
                   *[jax.ShapeDtypeStruct(c.shape, F32) for c in caches_t]),
        grid=(n_batch, DA_HEADS // pair),
        in_specs=qkv_specs + c_specs + [pl.BlockSpec((t_pad, LANES), lambda b, a: (0, 0))],
        out_specs=(pl.BlockSpec((t_pad, LANES), lambda b, a: (b, a)), *c_specs),
        compiler_params=_params(("arbitrary", "arbitrary")),
        name="da_sample",
    )(*([p_s] * (3 * N_DA_GROUPS)), *caches_t, sel)


def _route_kernel(tm_e, idx_ref, ltri_ref, utri_ref, dest_ref, be_ref, nused_ref,
                  rank_scr, carry_scr, pstart_scr):
    ph = pl.program_id(0)
    i = pl.program_id(1)
    tm = idx_ref.shape[0]
    lane = lax.broadcasted_iota(I32, (tm, LANES), 1)
    idx = idx_ref[...]
    hots = [lane == idx[:, k:k + 1] for k in range(TOP_K)]

    @pl.when(ph == 0)
    def _():
        @pl.when(i == 0)
        def _():
            carry_scr[...] = jnp.zeros_like(carry_scr)

        cnt = functools.reduce(jnp.add, [h.astype(F32) for h in hots])
        before = _dot(ltri_ref[...], cnt.astype(BF16)) + carry_scr[0:1, :]
        rank = jnp.zeros((tm, LANES), F32)
        for k in range(TOP_K):
            r_k = jnp.sum(jnp.where(hots[k], before, 0.0), axis=-1, keepdims=True)
            rank = jnp.where(lane == k, r_k, rank)
        rank_scr[i] = rank
        carry_scr[...] = carry_scr[...] + jnp.sum(cnt, axis=0, keepdims=True)

    @pl.when(ph == 1)
    def _():
        @pl.when(i == 0)
        def _():
            counts = carry_scr[...]
            padded = jnp.floor((counts + (tm_e - 1)) * (1.0 / tm_e)) * tm_e
            pend = jnp.dot(padded, utri_ref[...], preferred_element_type=F32, precision=HIGHEST)
            pstart_scr[...] = pend - padded
            nblk = be_ref.shape[0]
            start = lax.broadcasted_iota(I32, (nblk, LANES), 0).astype(F32) * tm_e
            is_e = lax.broadcasted_iota(I32, (nblk, LANES), 1) < N_EXPERTS
            done = jnp.where(jnp.logical_and(pend[0:1, :] <= start, is_e), 1.0, 0.0)
            be = jnp.minimum(jnp.sum(done, axis=-1, keepdims=True), N_EXPERTS - 1.0)
            be_ref[...] = jnp.broadcast_to(be, (nblk, LANES)).astype(I32)
            total = pend[:, N_EXPERTS - 1:N_EXPERTS] * (1.0 / tm_e)
            nused_ref[...] = jnp.broadcast_to(total, nused_ref.shape).astype(I32)

        dest = rank_scr[i]
        for k in range(TOP_K):
            d_k = jnp.sum(jnp.where(hots[k], pstart_scr[0:1, :], 0.0), axis=-1, keepdims=True)
            dest = dest + jnp.where(lane == k, d_k, 0.0)
        dest_ref[...] = dest.astype(I32)


def _route(idx_all, tm, tm_e, nblk):
    n = idx_all.shape[0]
    nt = n // tm
    r = jnp.arange(tm)
    ltri = (r[:, None] > r[None, :]).astype(BF16)
    c = jnp.arange(LANES)
    utri = ((c[:, None] <= c[None, :]) & (c[:, None] < N_EXPERTS)).astype(F32)
    nblk_pad = -(-nblk // SUBLANES) * SUBLANES
    return pl.pallas_call(
        functools.partial(_route_kernel, tm_e),
        out_shape=(jax.ShapeDtypeStruct((n, LANES), I32),
                   jax.ShapeDtypeStruct((nblk_pad, LANES), I32),
                   jax.ShapeDtypeStruct((SUBLANES, LANES), I32)),
        grid=(2, nt),
        in_specs=[pl.BlockSpec((tm, LANES), lambda ph, i: (i, 0)),
                  pl.BlockSpec((tm, tm), lambda ph, i: (0, 0)),
                  pl.BlockSpec((LANES, LANES), lambda ph, i: (0, 0))],
        out_specs=(pl.BlockSpec((tm, LANES), lambda ph, i: (i * ph, 0)),
                   pl.BlockSpec((nblk_pad, LANES), lambda ph, i: (0, 0)),
                   pl.BlockSpec((SUBLANES, LANES), lambda ph, i: (0, 0))),
        scratch_shapes=[pltpu.VMEM((nt, tm, LANES), F32), pltpu.VMEM((SUBLANES, LANES), F32),
                        pltpu.VMEM((SUBLANES, LANES), F32)],
        compiler_params=_params(("arbitrary", "arbitrary")),
        name="route",
    )(idx_all, ltri, utri)


def _row_copy_loops(tm, make_copy):
    def start(t, carry):
        for k in range(TOP_K):
            make_copy(t, k).start()
        return carry

    def wait(t, carry):
        for k in range(TOP_K):
            make_copy(t, k).wait()
        return carry

    lax.fori_loop(0, tm, start, 0)
    lax.fori_loop(0, tm, wait, 0)


def _dispatch_kernel(dest_ref, x_ref, xs_in_hbm, xs_hbm, sem):
    del xs_in_hbm
    tm = x_ref.shape[0]

    def make_copy(t, k):
        d = dest_ref[0, 0, t * TOP_K + k]
        return pltpu.make_async_copy(x_ref.at[pl.ds(t, 1)], xs_hbm.at[pl.ds(d, 1)], sem)

    _row_copy_loops(tm, make_copy)


def _dispatch(dest3, x2d, xs):
    nt, _, width = dest3.shape
    return pl.pallas_call(
        _dispatch_kernel,
        out_shape=jax.ShapeDtypeStruct(xs.shape, xs.dtype),
        grid=(nt,),
        in_specs=[pl.BlockSpec((1, 1, width), lambda i: (i, 0, 0), memory_space=pltpu.SMEM),
                  pl.BlockSpec((width // TOP_K, D_MODEL), lambda i: (i, 0)),
                  pl.BlockSpec(memory_space=pl.ANY)],
        out_specs=pl.BlockSpec(memory_space=pl.ANY),
        scratch_shapes=[pltpu.SemaphoreType.DMA(())],
        input_output_aliases={2: 0},
        compiler_params=_params(("arbitrary",)),
        name="dispatch",
    )(dest3, x2d, xs)


def _experts_kernel(be_ref, nused_ref, x_ref, wi_ref, bi_ref, wo_ref, bo_ref, o_ref,
                    wi_scr, wo_scr):
    i = pl.program_id(0)
    active = i < nused_ref[0]
    fresh = jnp.logical_or(i == 0, be_ref[i] != be_ref[jnp.maximum(i - 1, 0)])

    @pl.when(jnp.logical_and(active, fresh))
    def _():
        wi_scr[...] = wi_ref[0].astype(BF16)
        wo_scr[...] = wo_ref[0].astype(BF16)

    @pl.when(active)
    def _():
        hdn = _dot(x_ref[...].astype(BF16), wi_scr[...]) + bi_ref[0]
        gate = jnp.minimum(hdn[:, :D_FF], SWIGLU_LIMIT)
        up = jnp.clip(hdn[:, D_FF:], -SWIGLU_LIMIT, SWIGLU_LIMIT)
        act = (up + 1.0) * gate * _sigmoid(SWIGLU_ALPHA * gate)
        o_ref[...] = _dot(act.astype(BF16), wo_scr[...]) + bo_ref[0]

    @pl.when(jnp.logical_not(active))
    def _():
        o_ref[...] = jnp.zeros_like(o_ref)


def _experts(block_e, nused, xs, w_exp_in, b_exp_in, w_exp_out, b_exp_out, tm_e):
    n_rows = xs.shape[0]
    grid_spec = pltpu.PrefetchScalarGridSpec(
        num_scalar_prefetch=2,
        grid=(n_rows // tm_e,),
        in_specs=[
            pl.BlockSpec((tm_e, D_MODEL), lambda i, be, nu: (i, 0)),
            pl.BlockSpec((1, D_MODEL, 2 * D_FF), lambda i, be, nu: (be[i], 0, 0)),
            pl.BlockSpec((1, 1, 2 * D_FF), lambda i, be, nu: (be[i], 0, 0)),
            pl.BlockSpec((1, D_FF, D_MODEL), lambda i, be, nu: (be[i], 0, 0)),
            pl.BlockSpec((1, 1, D_MODEL), lambda i, be, nu: (be[i], 0, 0)),
        ],
        out_specs=pl.BlockSpec((tm_e, D_MODEL), lambda i, be, nu: (i, 0)),
        scratch_shapes=[pltpu.VMEM((D_MODEL, 2 * D_FF), BF16), pltpu.VMEM((D_FF, D_MODEL), BF16)],
    )
    return pl.pallas_call(
        _experts_kernel,
        out_shape=jax.ShapeDtypeStruct((n_rows, D_MODEL), F32),
        grid_spec=grid_spec,
        compiler_params=_params(("arbitrary",)),
        name="experts",
    )(block_e, nused, xs, w_exp_in, b_exp_in, w_exp_out, b_exp_out)


def _final_kernel(dest_ref, h1_ref, gate_ref, p_ref, wpi_ref, wpg_ref, ys_hbm, o_ref,
                  rows_scr, sem):
    tm = h1_ref.shape[0]

    def make_copy(t, k):
        d = dest_ref[0, 0, t * TOP_K + k]
        return pltpu.make_async_copy(ys_hbm.at[pl.ds(d, 1)], rows_scr.at[k, pl.ds(t, 1)], sem)

    _row_copy_loops(tm, make_copy)
    gates = gate_ref[...]
    y = functools.reduce(jnp.add, [gates[:, k:k + 1] * rows_scr[k] for k in range(TOP_K)])
    h2 = h1_ref[...] + y
    ms = jnp.mean(h2 * h2, axis=-1, keepdims=True)
    hn = (h2 * lax.rsqrt(ms + EPS)).astype(BF16)
    gate = _sigmoid(_dot(hn, wpg_ref[...]))
    o_ref[...] = h2 + gate * _dot(p_ref[...].astype(BF16), wpi_ref[...])


def _final(dest3, h1, gates, p2d, ys, prm):
    nt, _, width = dest3.shape
    tm = width // TOP_K
    row = lambda i: (i, 0)
    const = lambda i: (0, 0)
    return pl.pallas_call(
        _final_kernel,
        out_shape=jax.ShapeDtypeStruct(h1.shape, F32),
        grid=(nt,),
        in_specs=[pl.BlockSpec((1, 1, width), lambda i: (i, 0, 0), memory_space=pltpu.SMEM),
                  pl.BlockSpec((tm, D_MODEL), row),
                  pl.BlockSpec((tm, LANES), row),
                  pl.BlockSpec((tm, PLE_DIM), row),
                  pl.BlockSpec((PLE_DIM, D_MODEL), const),
                  pl.BlockSpec((D_MODEL, D_MODEL), const),
                  pl.BlockSpec(memory_space=pl.ANY)],
        out_specs=pl.BlockSpec((tm, D_MODEL), row),
        scratch_shapes=[pltpu.VMEM((TOP_K, tm, D_MODEL), F32), pltpu.SemaphoreType.DMA(())],
        compiler_params=_params(("arbitrary",)),
        name="final",
    )(dest3, h1, gates, p2d, prm["w_ple_in"], prm["w_ple_gate"], ys)


TOKEN_TILE = 256
PROJ_ROW_TILE = 1024
GLA_BLOCK = 256
EXPERT_TILE = 256
SAMPLE_PAD = SUBLANES


def kernel(x_prompt, x_sample, cache_da1_kv, cache_da2_kv, cache_da3_kv, state_gla, p_prompt, p_sample, norm_mix_g, w_in, w_gla_alpha, b_gla_alpha, gla_norm_g, w_gla_out, qk_norm_q, qk_norm_k, w_da_out, w_o, norm_moe_g, w_router, b_router, w_exp_in, b_exp_in, w_exp_out, b_exp_out, w_ple_in, w_ple_gate):
    depth = w_in.shape[0]
    assert depth == 1, "single-layer trunk"
    l = 0
    prm = _prep_params(norm_mix_g[l], w_in[l], w_gla_alpha[l], b_gla_alpha[l], gla_norm_g[l],
                       w_gla_out[l], qk_norm_q[l], qk_norm_k[l], w_da_out[l], w_o[l], norm_moe_g[l],
                       w_router[l], b_router[l], b_exp_in[l], b_exp_out[l], w_ple_in[l], w_ple_gate[l])
    n_b, seq, _ = x_prompt.shape
    d_b, t_new, _ = x_sample.shape
    caches = (cache_da1_kv[l], cache_da2_kv[l], cache_da3_kv[l])
    assert seq % (DA_SPAN * DA_DILATIONS[-1]) == 0 and seq % PROJ_ROW_TILE == 0
    assert t_new <= SAMPLE_PAD and (d_b * SAMPLE_PAD) % TOKEN_TILE == 0
    for g in range(N_DA_GROUPS):
        assert caches[g].shape[1] == DA_WINDOWS[g]

    hp = x_prompt.reshape(n_b * seq, D_MODEL)
    pp, la_p = _proj(hp, prm, PROJ_ROW_TILE, dilate=True)
    zero_state = jnp.zeros((n_b, GLA_HEADS, GLA_DK, GLA_DV), F32)
    og_p, state_p = _gla(pp, la_p, zero_state, prm["gla_norm_g"], n_b, seq, GLA_BLOCK, GLA_CHUNK)
    da_p = [_da_prompt(pp, g, n_b, seq, PROJ_ROW_TILE) for g in range(N_DA_GROUPS)]
    h1_p, xn_p, idx_p, gate_p = _mix(pp, og_p, [o for o, _ in da_p], [s for _, s in da_p], hp, prm,
                                     TOKEN_TILE, PROJ_ROW_TILE)

    pad_t = ((0, 0), (0, SAMPLE_PAD - t_new), (0, 0))
    hs = jnp.pad(x_sample, pad_t).reshape(d_b * SAMPLE_PAD, D_MODEL)
    ps, la_s = _proj(hs, prm, TOKEN_TILE, dilate=False)
    og_s, state_s = _gla(ps, la_s, state_gla[l], prm["gla_norm_g"], d_b, SAMPLE_PAD, SAMPLE_PAD,
                         SAMPLE_PAD, t_valid=t_new)
    caches_t = [jnp.transpose(c, (0, 2, 3, 4, 1)) for c in caches]
    o_s, *new_caches_t = _da_sample(ps, caches_t, d_b, SAMPLE_PAD, t_new)
    h1_s, xn_s, idx_s, gate_s = _mix(ps, og_s, [o_s], [], hs, prm, TOKEN_TILE)

    n_p, n_s = hp.shape[0], hs.shape[0]
    n_rows = (n_p + n_s) * TOP_K + N_EXPERTS * EXPERT_TILE
    dest, block_e, nused = _route(jnp.concatenate([idx_p, idx_s], axis=0), TOKEN_TILE, EXPERT_TILE,
                                  n_rows // EXPERT_TILE)
    dest3 = dest[:, :TOP_K].reshape(-1, 1, TOKEN_TILE * TOP_K)
    dest3_p, dest3_s = dest3[:n_p // TOKEN_TILE], dest3[n_p // TOKEN_TILE:]
    xs = jnp.zeros((n_rows, D_MODEL), F32)
    xs = _dispatch(dest3_p, xn_p, xs)
    xs = _dispatch(dest3_s, xn_s, xs)
    ys = _experts(block_e[:n_rows // EXPERT_TILE, 0], nused[0, :1], xs, w_exp_in[l], prm["b_exp_in"],
                  w_exp_out[l], prm["b_exp_out"], EXPERT_TILE)
    out_p = _final(dest3_p, h1_p, gate_p, p_prompt[l].reshape(n_p, PLE_DIM), ys, prm)
    out_s = _final(dest3_s, h1_s, gate_s, jnp.pad(p_sample[l], pad_t).reshape(n_s, PLE_DIM), ys, prm)

    y_prompt = out_p.reshape(n_b, seq, D_MODEL)
    y_sample = out_s.reshape(d_b, SAMPLE_PAD, D_MODEL)[:, :t_new]
    outs_p, outs_s = [], []
    for g in range(N_DA_GROUPS):
        win, dil = min(DA_WINDOWS[g], seq), DA_DILATIONS[g]
        ck = (COL_DA + 3 * g + 1) * PROJ_TN
        n_tiles = -(-win // PROJ_ROW_TILE)
        kv_p = pp.reshape(n_b, seq // PROJ_ROW_TILE, dil, PROJ_ROW_TILE // dil, PROJ_WIDTH)
        kv_p = kv_p[:, -n_tiles:, :, :, ck:ck + 2 * DA_WIDTH]
        kv_p = jnp.transpose(kv_p, (0, 1, 3, 2, 4)).reshape(n_b, n_tiles * PROJ_ROW_TILE, -1)
        outs_p.append(kv_p[:, -win:].reshape(1, n_b, win, 2, DA_HEADS, DA_HEAD_DIM))
        outs_s.append(jnp.transpose(new_caches_t[g], (0, 4, 1, 2, 3))[None])
    return (y_prompt, y_sample, state_p[None], state_s[None], outs_p[0], outs_s[0], outs_p[1],
            outs_s[1], outs_p[2], outs_s[2])
```

```python
import functools

import jax
import jax.numpy as jnp
from jax import lax
from jax.experimental import pallas as pl
from jax.experimental.pallas import tpu as pltpu

F32 = jnp.float32
BF16 = jnp.bfloat16
I32 = jnp.int32

D_MODEL = 1024
GLA_HEADS = 4
GLA_DK = 128
GLA_DV = 256
GLA_QK = GLA_HEADS * GLA_DK
GLA_V = GLA_HEADS * GLA_DV
GLA_RANK = 16
GLA_TAU = 16.0
GLA_CHUNK = 32
DA_WINDOWS = (128, 512, 2048)
DA_DILATIONS = (1, 4, 16)
DA_SPAN = 128
N_DA_GROUPS = 3
DA_HEADS = 8
DA_HEAD_DIM = 64
DA_WIDTH = DA_HEADS * DA_HEAD_DIM
N_EXPERTS = 32
TOP_K = 4
D_FF = D_MODEL
SWIGLU_LIMIT = 7.0
SWIGLU_ALPHA = 1.702
PLE_DIM = 256
EPS = 1e-6

LANES = 128
SUBLANES = 8
VMEM_LIMIT = 56 * 1024 * 1024

PROJ_TN = 512
COL_GQ, COL_GK, COL_GV, COL_GR, COL_GA, COL_GB, COL_DA = 0, 1, 2, 4, 6, 8, 10
PROJ_TILES = COL_DA + 3 * N_DA_GROUPS
PROJ_WIDTH = PROJ_TILES * PROJ_TN
NEG_BIG = -1e30

HIGHEST = lax.Precision.HIGHEST


def _dot(a, b):
    return jnp.dot(a, b, preferred_element_type=F32)


def _dot_nt(a, b):
    return lax.dot_general(a, b, (((1,), (1,)), ((), ())), preferred_element_type=F32)


def _dot_tn(a, b):
    return lax.dot_general(a, b, (((0,), (0,)), ((), ())), preferred_element_type=F32)


def _sigmoid(x):
    return 1.0 / (1.0 + jnp.exp(-x))


def _params(sem, vmem=VMEM_LIMIT):
    return pltpu.CompilerParams(dimension_semantics=sem, vmem_limit_bytes=vmem)


assert D_MODEL == SUBLANES * LANES


def _row_tile(r):
    return pl.ds(pl.multiple_of(r * SUBLANES, SUBLANES), SUBLANES)


def _store_row_tiles(ref, value, lead=()):
    rows = value.shape[0]
    for s in range(SUBLANES):
        ref[(*lead, pl.ds(s, rows, stride=SUBLANES), slice(None))] = value[:, s * LANES:(s + 1) * LANES]


def _load_row_tiles(ref, rows, lead=()):
    return jnp.concatenate(
        [ref[(*lead, pl.ds(s, rows, stride=SUBLANES), slice(None))] for s in range(SUBLANES)], axis=1)


def _proj_kernel(dilate, x_ref, g_ref, w_ref, wlr_ref, wal_ref, bal_ref, qkg_ref, bd_ref,
                 p_ref, la_ref, xn_scr, stage_scr):
    j = pl.program_id(1)
    tm = x_ref.shape[0]

    @pl.when(j == 0)
    def _():
        x = x_ref[...]
        ms = jnp.mean(x * x, axis=-1, keepdims=True)
        xn = (x * lax.rsqrt(ms + EPS) * g_ref[...]).astype(BF16)
        xn_scr[...] = xn
        glr = _dot(xn, wlr_ref[...])
        z = _dot(glr.astype(BF16), wal_ref[...]) + bal_ref[...]
        log_sig = jnp.minimum(z, 0.0) - jnp.log(1.0 + jnp.exp(-jnp.abs(z)))
        la_ref[...] = log_sig * (1.0 / GLA_TAU)

    acc = _dot(xn_scr[...], w_ref[...])
    is_qk = jnp.logical_and(j >= COL_DA, (j - COL_DA) % 3 != 2)
    group = (j - COL_DA) // 3
    staged = [jnp.logical_and(j >= COL_DA, group == g) if dilate and DA_DILATIONS[g] > 1 else None
              for g in range(N_DA_GROUPS)]
    any_staged = functools.reduce(jnp.logical_or, [s for s in staged if s is not None], False)

    def qk_normed():
        sq = acc * acc
        hi = sq.astype(BF16)
        lo = (sq - hi.astype(F32)).astype(BF16)
        ssum = _dot(hi, bd_ref[...]) + _dot(lo, bd_ref[...])
        return acc * lax.rsqrt(ssum * (1.0 / DA_HEAD_DIM) + EPS) * qkg_ref[...]

    def emit(value):
        if any_staged is False:
            p_ref[...] = value
            return

        @pl.when(any_staged)
        def _():
            for c in range(PROJ_TN // LANES):
                stage_scr[c] = value[:, c * LANES:(c + 1) * LANES]

        @pl.when(jnp.logical_not(any_staged))
        def _():
            p_ref[...] = value

    @pl.when(is_qk)
    def _():
        emit(qk_normed())

    @pl.when(jnp.logical_not(is_qk))
    def _():
        emit(acc)

    for g in range(N_DA_GROUPS):
        if staged[g] is None:
            continue
        dil = DA_DILATIONS[g]
        per = tm // dil

        @pl.when(staged[g])
        def _(dil=dil, per=per):
            for r in range(dil):
                for c in range(PROJ_TN // LANES):
                    p_ref[r * per:(r + 1) * per, c * LANES:(c + 1) * LANES] = (
                        stage_scr[c, pl.ds(r, per, stride=dil), :])


def _proj(x2d, prm, tm, dilate):
    m = x2d.shape[0]
    grid = (m // tm, PROJ_TILES)
    const = lambda i, j: (0, 0)
    return pl.pallas_call(
        functools.partial(_proj_kernel, dilate),
        out_shape=(jax.ShapeDtypeStruct((m, PROJ_WIDTH), F32),
                   jax.ShapeDtypeStruct((m, GLA_QK), F32)),
        grid=grid,
        in_specs=[
            pl.BlockSpec((tm, D_MODEL), lambda i, j: (i, 0)),
            pl.BlockSpec((1, D_MODEL), const),
            pl.BlockSpec((D_MODEL, PROJ_TN), lambda i, j: (0, j)),
            pl.BlockSpec((D_MODEL, LANES), const),
            pl.BlockSpec((LANES, GLA_QK), const),
            pl.BlockSpec((1, GLA_QK), const),
            pl.BlockSpec((1, PROJ_TN), lambda i, j: (0, jnp.maximum(j - COL_DA, 0))),
            pl.BlockSpec((PROJ_TN, PROJ_TN), const),
        ],
        out_specs=(pl.BlockSpec((tm, PROJ_TN), lambda i, j: (i, j)),
                   pl.BlockSpec((tm, GLA_QK), lambda i, j: (i, 0))),
        scratch_shapes=[pltpu.VMEM((tm, D_MODEL), BF16),
                        pltpu.VMEM((PROJ_TN // LANES, tm, LANES), F32)],
        compiler_params=_params(("arbitrary", "arbitrary")),
        name="proj",
    )(x2d, prm["norm_mix_g"], prm["w_main"], prm["w_lr"], prm["w_alpha"], prm["b_alpha"],
      prm["qk_gain"], prm["head_ones"])


def _gla_kernel(chunk, t_valid, q_ref, k_ref, v_ref, r_ref, la_ref, s0_ref, g_ref, tri_ref,
                o_ref, s_ref, s_scr):
    t = pl.program_id(1)
    tb = q_ref.shape[0]
    n_chunks = tb // chunk
    row_ok = lax.broadcasted_iota(I32, (tb, GLA_DK), 0) < t_valid

    @pl.when(t == 0)
    def _():
        for h in range(GLA_HEADS):
            s_scr[h] = s0_ref[0, h].T

    tri = tri_ref[...]
    causal = tri > 0
    for h in range(GLA_HEADS):
        ks = slice(h * GLA_DK, (h + 1) * GLA_DK)
        vs = slice(h * GLA_DV, (h + 1) * GLA_DV)
        la = la_ref[:, ks]
        if t_valid < tb:
            la = jnp.where(row_ok, la, 0.0)
        la_hi = la.astype(BF16)
        la_lo = (la - la_hi.astype(F32)).astype(BF16)
        cum = _dot(tri, la_hi) + _dot(tri, la_lo)
        q = q_ref[:, ks] * (GLA_DK ** -0.5)
        k = k_ref[:, ks]
        v = v_ref[:, vs].astype(BF16)
        q_dec = (q * jnp.exp(cum)).astype(BF16)
        k_inv = (k * jnp.exp(-cum)).astype(BF16)
        att = jnp.where(causal, _dot_nt(q_dec, k_inv), 0.0)
        o_intra = _dot(att.astype(BF16), v)
        s_t = s_scr[h]
        for c in range(n_chunks):
            rows = slice(c * chunk, (c + 1) * chunk)
            tot = cum[(c + 1) * chunk - 1:(c + 1) * chunk, :]
            k_end = (k[rows] * jnp.exp(tot - cum[rows])).astype(BF16)
            o_c = o_intra[rows] + _dot_nt(q_dec[rows], s_t.astype(BF16))
            s_t = s_t * jnp.exp(tot) + _dot_tn(v[rows], k_end)
            ms = jnp.mean(o_c * o_c, axis=-1, keepdims=True)
            gate = r_ref[rows, vs]
            o_ref[rows, vs] = o_c * lax.rsqrt(ms + EPS) * g_ref[...] * (gate * _sigmoid(gate))
        s_scr[h] = s_t

    @pl.when(t == pl.num_programs(1) - 1)
    def _():
        for h in range(GLA_HEADS):
            s_ref[0, h] = s_scr[h].T


def _gla(p, la, s0, gla_norm_g, n_batch, t_len, tb, chunk, t_valid=None):
    nt = t_len // tb
    t_valid = tb if t_valid is None else t_valid
    r = jnp.arange(tb)
    same = (r[:, None] // chunk) == (r[None, :] // chunk)
    tri = (same & (r[:, None] >= r[None, :])).astype(BF16)
    row = lambda b, t: b * nt + t
    const = lambda b, t: (0, 0)
    return pl.pallas_call(
        functools.partial(_gla_kernel, chunk, t_valid),
        out_shape=(jax.ShapeDtypeStruct((n_batch * t_len, GLA_V), F32),
                   jax.ShapeDtypeStruct((n_batch, GLA_HEADS, GLA_DK, GLA_DV), F32)),
        grid=(n_batch, nt),
        in_specs=[
            pl.BlockSpec((tb, GLA_QK), lambda b, t: (row(b, t), COL_GQ)),
            pl.BlockSpec((tb, GLA_QK), lambda b, t: (row(b, t), COL_GK)),
            pl.BlockSpec((tb, GLA_V), lambda b, t: (row(b, t), COL_GV * PROJ_TN // GLA_V)),
            pl.BlockSpec((tb, GLA_V), lambda b, t: (row(b, t), COL_GR * PROJ_TN // GLA_V)),
            pl.BlockSpec((tb, GLA_QK), lambda b, t: (row(b, t), 0)),
            pl.BlockSpec((1, GLA_HEADS, GLA_DK, GLA_DV), lambda b, t: (b, 0, 0, 0)),
            pl.BlockSpec((1, GLA_DV), const),
            pl.BlockSpec((tb, tb), const),
        ],
        out_specs=(pl.BlockSpec((tb, GLA_V), lambda b, t: (row(b, t), 0)),
                   pl.BlockSpec((1, GLA_HEADS, GLA_DK, GLA_DV), lambda b, t: (b, 0, 0, 0))),
        scratch_shapes=[pltpu.VMEM((GLA_HEADS, GLA_DV, GLA_DK), F32)],
        compiler_params=_params(("arbitrary", "arbitrary")),
        name="gla",
    )(p, p, p, p, la, s0, gla_norm_g, tri)


def _prep_params(norm_mix_g, w_in, w_gla_alpha, b_gla_alpha, gla_norm_g, w_gla_out, qk_norm_q,
                 qk_norm_k, w_da_out, w_o, norm_moe_g, w_router, b_router, b_exp_in, b_exp_out,
                 w_ple_in, w_ple_gate):
    o_lr = GLA_QK * 2 + GLA_V * 2
    o_da = o_lr + GLA_RANK
    o_ga = o_da + 3 * N_DA_GROUPS * DA_WIDTH
    w_main = jnp.concatenate([w_in[:, :o_lr], w_in[:, o_ga:], w_in[:, o_da:o_ga]], axis=1).astype(BF16)
    w_lr = jnp.pad(w_in[:, o_lr:o_da], ((0, 0), (0, LANES - GLA_RANK))).astype(BF16)
    w_alpha = jnp.pad(w_gla_alpha, ((0, LANES - GLA_RANK), (0, 0))).astype(BF16)
    gains = []
    for g in range(N_DA_GROUPS):
        gains.append(jnp.tile(qk_norm_q[g], DA_HEADS) * (DA_HEAD_DIM ** -0.5))
        gains.append(jnp.tile(qk_norm_k[g], DA_HEADS))
        gains.append(jnp.ones((DA_WIDTH,), F32))
    qk_gain = jnp.concatenate(gains)[None, :]
    c = jnp.arange(PROJ_TN)
    head_ones = (c[:, None] // DA_HEAD_DIM == c[None, :] // DA_HEAD_DIM).astype(BF16)
    pad_e = LANES - N_EXPERTS
    w_r = jnp.pad(w_router, ((0, 0), (0, pad_e)))
    w_r_hi = w_r.astype(BF16)
    w_r_split = jnp.stack([w_r_hi, (w_r - w_r_hi.astype(F32)).astype(BF16)])
    return dict(
        norm_mix_g=norm_mix_g[None, :], w_main=w_main, w_lr=w_lr, w_alpha=w_alpha,
        b_alpha=b_gla_alpha[None, :], qk_gain=qk_gain, head_ones=head_ones,
        gla_norm_g=gla_norm_g[None, :], w_gla_out=w_gla_out.astype(BF16),
        w_da_out=w_da_out.astype(BF16), w_o=w_o.astype(BF16), norm_moe_g=norm_moe_g[None, :],
        w_router=w_r_split,
        b_router=jnp.pad(b_router, (0, pad_e), constant_values=NEG_BIG)[None, :],
        b_exp_in=b_exp_in[:, None, :], b_exp_out=b_exp_out[:, None, :],
        w_ple_in=w_ple_in.astype(BF16), w_ple_gate=w_ple_gate.astype(BF16),
    )


def _da_prompt_kernel(q_ref, kc_ref, kp_ref, vc_ref, vp_ref, bias_ref, o_ref, lse_ref):
    bias = bias_ref[0]
    low = lax.broadcasted_iota(I32, (DA_SPAN, LANES), 1) < DA_HEAD_DIM
    pieces = q_ref.shape[1:4:2]

    def rows(ref, cs):
        return ref[0, :, 0, :, cs].reshape(DA_SPAN, LANES)

    for a in range(DA_WIDTH // LANES):
        cs = slice(a * LANES, (a + 1) * LANES)
        q2 = rows(q_ref, cs)
        k2 = jnp.concatenate([rows(kp_ref, cs), rows(kc_ref, cs)], axis=0).astype(BF16)
        v2 = jnp.concatenate([rows(vp_ref, cs), rows(vc_ref, cs)], axis=0).astype(BF16)
        res = []
        for sel in (low, jnp.logical_not(low)):
            s = _dot_nt(jnp.where(sel, q2, 0.0).astype(BF16), k2) + bias
            m = jnp.max(s, axis=-1, keepdims=True)
            p = jnp.exp(s - m)
            l = jnp.sum(p, axis=-1, keepdims=True)
            res.append((_dot(p.astype(BF16), v2) / l, m + jnp.log(l)))
        o_ref[0, :, 0, :, cs] = jnp.where(low, res[0][0], res[1][0]).reshape(*pieces, LANES)
        lse_ref[0, :, 0, :, cs] = jnp.where(low, res[0][1], res[1][1]).reshape(*pieces, LANES)


def _da_bias():
    qi = jnp.arange(DA_SPAN)[:, None]
    kj = jnp.arange(2 * DA_SPAN)[None, :]
    d = kj - qi
    band = (d >= 0) & (d <= DA_SPAN)
    first = band & (kj >= DA_SPAN)
    return jnp.where(jnp.stack([first, band]), 0.0, NEG_BIG).astype(F32)


def _da_prompt(p, g, n_batch, seq, tile):
    dil = DA_DILATIONS[g]
    per = tile // dil
    nb = seq // dil // DA_SPAN
    p5 = p.reshape(n_batch, seq // tile, dil, per, PROJ_WIDTH)
    cq, ck, cv = (COL_DA + 3 * g + i for i in range(3))
    if per >= DA_SPAN:
        assert per % DA_SPAN == 0
        blk = (1, 1, 1, DA_SPAN, DA_WIDTH)
        place = lambda n: (n // (per // DA_SPAN), n % (per // DA_SPAN))
    else:
        assert DA_SPAN % per == 0 and per % SUBLANES == 0
        blk = (1, DA_SPAN // per, 1, per, DA_WIDTH)
        place = lambda n: (n, 0)

    def spec(c, back):
        def index(b, r, n):
            tile_i, row_i = place(jnp.maximum(n - back, 0))
            return (b, tile_i, r, row_i, c)
        return pl.BlockSpec(blk, index)

    out_sds = jax.ShapeDtypeStruct((n_batch, seq // tile, dil, per, DA_WIDTH), F32)
    return pl.pallas_call(
        _da_prompt_kernel,
        out_shape=(out_sds, out_sds),
        grid=(n_batch, dil, nb),
        in_specs=[spec(cq, 0), spec(ck, 0), spec(ck, 1), spec(cv, 0), spec(cv, 1),
                  pl.BlockSpec((1, DA_SPAN, 2 * DA_SPAN), lambda b, r, n: (jnp.minimum(n, 1), 0, 0))],
        out_specs=(spec(0, 0), spec(0, 0)),
        compiler_params=_params(("arbitrary", "arbitrary", "arbitrary")),
        name=f"da_prompt{g}",
    )(p5, p5, p5, p5, p5, _da_bias())


def _mix_kernel(dils, *refs):
    og_ref = refs[0]
    n_groups = len(dils) if dils else 1
    n_lse = n_groups if dils else 0
    o_refs = refs[1:1 + n_groups]
    l_refs = refs[1 + n_groups:1 + n_groups + n_lse]
    (ga_ref, gb_ref, h_ref, wga_ref, wda_ref, wo_ref, gm_ref, wr_ref, br_ref,
     h1_ref, xn_ref, idx_ref, gate_ref) = refs[1 + n_groups + n_lse:14 + n_groups + n_lse]
    scratch = iter(refs[14 + n_groups + n_lse:])

    def token_major(ref, dil):
        if dil == 1:
            return ref[0, 0]
        scr = next(scratch)
        per = ref.shape[2]
        for r in range(dil):
            for c in range(DA_WIDTH // LANES):
                scr[c, pl.ds(r, per, stride=dil), :] = ref[0, r, :, c * LANES:(c + 1) * LANES]
        return jnp.concatenate([scr[c] for c in range(DA_WIDTH // LANES)], axis=1)

    if not dils:
        o_da = o_refs[0][...]
    else:
        outs = [token_major(r, d) for r, d in zip(o_refs, dils)]
        lses = [token_major(r, d) for r, d in zip(l_refs, dils)]
        m = functools.reduce(jnp.maximum, lses)
        es = [jnp.exp(l - m) for l in lses]
        num = functools.reduce(jnp.add, [e * o for e, o in zip(es, outs)])
        o_da = num / functools.reduce(jnp.add, es)
    y_a = _dot(og_ref[...].astype(BF16), wga_ref[...])
    y_b = _dot(o_da.astype(BF16), wda_ref[...])
    y = _sigmoid(ga_ref[...]) * y_a + _sigmoid(gb_ref[...]) * y_b
    h1 = h_ref[...] + _dot(y.astype(BF16), wo_ref[...])
    h1_ref[...] = h1
    ms = jnp.mean(h1 * h1, axis=-1, keepdims=True)
    xn = h1 * lax.rsqrt(ms + EPS) * gm_ref[...]
    _store_row_tiles(xn_ref, xn)
    xn_hi = xn.astype(BF16)
    xn_lo = (xn - xn_hi.astype(F32)).astype(BF16)
    logits = (_dot(xn_hi, wr_ref[0]) + _dot(xn_lo, wr_ref[0]) + _dot(xn_hi, wr_ref[1])) + br_ref[...]
    lane = lax.broadcasted_iota(I32, logits.shape, 1)
    lane_f = lane.astype(F32)
    work = logits
    idx_out = jnp.zeros(logits.shape, I32)
    val_out = jnp.zeros(logits.shape, F32)
    vals = []
    for k in range(TOP_K):
        top = jnp.max(work, axis=-1, keepdims=True)
        pick = jnp.min(jnp.where(work == top, lane_f, float(LANES)), axis=-1, keepdims=True)
        pick = pick.astype(I32)
        vals.append(top)
        idx_out = jnp.where(lane == k, pick, idx_out)
        work = jnp.where(lane == pick, -3e38, work)
    es = [jnp.exp(v - vals[0]) for v in vals]
    den = functools.reduce(jnp.add, es)
    for k in range(TOP_K):
        val_out = jnp.where(lane == k, es[k] / den, val_out)
    idx_ref[...] = idx_out
    gate_ref[...] = val_out


def _mix(p, o_gla, o_das, lses, h2d, prm, tm, tile=None):
    m = h2d.shape[0]
    row = lambda i: (i, 0)
    const = lambda i: (0, 0)
    wide = pl.BlockSpec((tm, D_MODEL), row)
    small = pl.BlockSpec((tm, LANES), row)
    if lses:
        dils = DA_DILATIONS
        sub = tile // tm
        da_in, da_specs, scratch = [], [], []
        for arr, dil in list(zip(o_das, dils)) + list(zip(lses, dils)):
            assert (tm // dil) % SUBLANES == 0
            da_in.append(arr.reshape(-1, dil, tile // dil, DA_WIDTH))
            da_specs.append(pl.BlockSpec((1, dil, tm // dil, DA_WIDTH),
                                         lambda i: (i // sub, 0, i % sub, 0)))
            if dil > 1:
                scratch.append(pltpu.VMEM((DA_WIDTH // LANES, tm, LANES), F32))
    else:
        dils, da_in, scratch = None, list(o_das), []
        da_specs = [pl.BlockSpec((tm, DA_WIDTH), row)]
    return pl.pallas_call(
        functools.partial(_mix_kernel, dils),
        out_shape=(jax.ShapeDtypeStruct((m, D_MODEL), F32),
                   jax.ShapeDtypeStruct((m * SUBLANES, LANES), F32),
                   jax.ShapeDtypeStruct((m, LANES), I32), jax.ShapeDtypeStruct((m, LANES), F32)),
        grid=(m // tm,),
        scratch_shapes=scratch,
        in_specs=[wide] + da_specs + [
            pl.BlockSpec((tm, D_MODEL), lambda i: (i, COL_GA * PROJ_TN // D_MODEL)),
            pl.BlockSpec((tm, D_MODEL), lambda i: (i, COL_GB * PROJ_TN // D_MODEL)),
            wide,
            pl.BlockSpec((GLA_V, D_MODEL), const),
            pl.BlockSpec((DA_WIDTH, D_MODEL), const),
            pl.BlockSpec((D_MODEL, D_MODEL), const),
            pl.BlockSpec((1, D_MODEL), const),
            pl.BlockSpec((2, D_MODEL, LANES), lambda i: (0, 0, 0)),
            pl.BlockSpec((1, LANES), const),
        ],
        out_specs=(wide, pl.BlockSpec((tm * SUBLANES, LANES), row), small, small),
        compiler_params=_params(("arbitrary",)),
        name="mix",
    )(o_gla, *da_in, p, p, h2d, prm["w_gla_out"], prm["w_da_out"], prm["w_o"],
      prm["norm_moe_g"], prm["w_router"], prm["b_router"])


def _da_sample_kernel(t_new, *refs):
    qkv = [refs[3 * g:3 * g + 3] for g in range(N_DA_GROUPS)]
    c_refs = refs[9:12]
    sel_ref = refs[12]
    o_ref = refs[13]
    new_refs = refs[14:17]
    t_pad = o_ref.shape[0]
    q_row = lax.broadcasted_iota(I32, (t_pad, t_pad), 0)
    k_row = lax.broadcasted_iota(I32, (t_pad, t_pad), 1)
    lane = lax.broadcasted_iota(I32, (DA_HEAD_DIM, LANES), 1)
    heads = []
    for hh in range(LANES // DA_HEAD_DIM):
        hs = slice(hh * DA_HEAD_DIM, (hh + 1) * DA_HEAD_DIM)
        lses, accs, dens = [], [], []
        for g in range(N_DA_GROUPS):
            dil, win = DA_DILATIONS[g], DA_WINDOWS[g]
            q_ref, k_ref, v_ref = qkv[g]
            q = q_ref[:, hs].astype(BF16)
            k_new = k_ref[:, hs]
            v_new = v_ref[:, hs]
            kt = c_refs[g][0, 0, hh]
            vt = c_refs[g][0, 1, hh]
            t_idx = lax.broadcasted_iota(I32, (t_pad, win), 0)
            off = lax.broadcasted_iota(I32, (t_pad, win), 1) - t_idx
            ok = jnp.logical_and(off >= 0, jnp.bitwise_and(off, dil - 1) == 0)
            s = jnp.where(ok, _dot(q, kt.astype(BF16)), NEG_BIG)
            back = q_row - k_row
            ok_new = jnp.logical_and(jnp.logical_and(back >= 0, jnp.bitwise_and(back, dil - 1) == 0),
                                     k_row < t_new)
            s_new = jnp.where(ok_new, _dot_nt(q, k_new.astype(BF16)), NEG_BIG)
            m = jnp.maximum(jnp.max(s, axis=-1, keepdims=True), jnp.max(s_new, axis=-1, keepdims=True))
            p = jnp.exp(s - m)
            p_new = jnp.exp(s_new - m)
            dens.append(jnp.sum(p, axis=-1, keepdims=True) + jnp.sum(p_new, axis=-1, keepdims=True))
            accs.append(_dot_nt(p.astype(BF16), vt.astype(BF16))
                        + _dot(p_new.astype(BF16), v_new.astype(BF16)))
            lses.append(m + jnp.log(dens[-1]))
            for kv, (old, new) in enumerate(((kt, k_new), (vt, v_new))):
                moved = pltpu.roll(old, win - t_new, axis=1)
                tail = lax.dot_general(new, sel_ref[...], (((0,), (0,)), ((), ())),
                                       preferred_element_type=F32, precision=HIGHEST)
                tail = jnp.where(lane >= LANES - t_new, tail, moved[:, win - LANES:])
                if win > LANES:
                    new_refs[g][0, kv, hh, :, :win - LANES] = moved[:, :win - LANES]
                new_refs[g][0, kv, hh, :, win - LANES:] = tail
        top = functools.reduce(jnp.maximum, lses)
        ws = [jnp.exp(l - top) for l in lses]
        wsum = functools.reduce(jnp.add, ws)
        heads.append(functools.reduce(
            jnp.add, [accs[g] * (ws[g] / (wsum * dens[g])) for g in range(N_DA_GROUPS)]))
    o_ref[...] = jnp.concatenate(heads, axis=1)


def _da_sample(p_s, caches_t, n_batch, t_pad, t_new):
    pair = LANES // DA_HEAD_DIM
    t = jnp.arange(t_pad)[:, None]
    sel = ((jnp.arange(LANES)[None, :] == LANES - t_new + t) & (t < t_new)).astype(F32)
    per_tile = PROJ_TN // LANES
    qkv_specs = [pl.BlockSpec((t_pad, LANES), lambda b, a, col=COL_DA + i: (b, col * per_tile + a))
                 for i in range(3 * N_DA_GROUPS)]
    c_specs = [pl.BlockSpec((1, 2, pair, DA_HEAD_DIM, DA_WINDOWS[g]), lambda b, a: (b, 0, a, 0, 0))
               for g in range(N_DA_GROUPS)]
    return pl.pallas_call(
        functools.partial(_da_sample_kernel, t_new),
        out_shape=(jax.ShapeDtypeStruct((n_batch * t_pad, DA_WIDTH), F32),
                   *[jax.ShapeDtypeStruct(c.shape, F32) for c in caches_t]),
        grid=(n_batch, DA_HEADS // pair),
        in_specs=qkv_specs + c_specs + [pl.BlockSpec((t_pad, LANES), lambda b, a: (0, 0))],
        out_specs=(pl.BlockSpec((t_pad, LANES), lambda b, a: (b, a)), *c_specs),
        compiler_params=_params(("arbitrary", "arbitrary")),
        name="da_sample",
    )(*([p_s] * (3 * N_DA_GROUPS)), *caches_t, sel)


def _route_kernel(tm_e, idx_ref, ltri_ref, utri_ref, dest_ref, be_ref, nused_ref,
                  rank_scr, carry_scr, pstart_scr):
    ph = pl.program_id(0)
    i = pl.program_id(1)
    tm = idx_ref.shape[0]
    lane = lax.broadcasted_iota(I32, (tm, LANES), 1)
    idx = idx_ref[...]
    hots = [lane == idx[:, k:k + 1] for k in range(TOP_K)]

    @pl.when(ph == 0)
    def _():
        @pl.when(i == 0)
        def _():
            carry_scr[...] = jnp.zeros_like(carry_scr)

        cnt = functools.reduce(jnp.add, [h.astype(F32) for h in hots])
        before = _dot(ltri_ref[...], cnt.astype(BF16)) + carry_scr[0:1, :]
        rank = jnp.zeros((tm, LANES), F32)
        for k in range(TOP_K):
            r_k = jnp.sum(jnp.where(hots[k], before, 0.0), axis=-1, keepdims=True)
            rank = jnp.where(lane == k, r_k, rank)
        rank_scr[i] = rank
        carry_scr[...] = carry_scr[...] + jnp.sum(cnt, axis=0, keepdims=True)

    @pl.when(ph == 1)
    def _():
        @pl.when(i == 0)
        def _():
            counts = carry_scr[...]
            padded = jnp.floor((counts + (tm_e - 1)) * (1.0 / tm_e)) * tm_e
            pend = jnp.dot(padded, utri_ref[...], preferred_element_type=F32, precision=HIGHEST)
            pstart_scr[...] = pend - padded
            nblk = be_ref.shape[0]
            start = lax.broadcasted_iota(I32, (nblk, LANES), 0).astype(F32) * tm_e
            is_e = lax.broadcasted_iota(I32, (nblk, LANES), 1) < N_EXPERTS
            done = jnp.where(jnp.logical_and(pend[0:1, :] <= start, is_e), 1.0, 0.0)
            be = jnp.minimum(jnp.sum(done, axis=-1, keepdims=True), N_EXPERTS - 1.0)
            be_ref[...] = jnp.broadcast_to(be, (nblk, LANES)).astype(I32)
            total = pend[:, N_EXPERTS - 1:N_EXPERTS] * (1.0 / tm_e)
            nused_ref[...] = jnp.broadcast_to(total, nused_ref.shape).astype(I32)

        dest = rank_scr[i]
        for k in range(TOP_K):
            d_k = jnp.sum(jnp.where(hots[k], pstart_scr[0:1, :], 0.0), axis=-1, keepdims=True)
            dest = dest + jnp.where(lane == k, d_k, 0.0)
        dest_ref[...] = dest.astype(I32)


def _route(idx_all, tm, tm_e, nblk):
    n = idx_all.shape[0]
    nt = n // tm
    r = jnp.arange(tm)
    ltri = (r[:, None] > r[None, :]).astype(BF16)
    c = jnp.arange(LANES)
    utri = ((c[:, None] <= c[None, :]) & (c[:, None] < N_EXPERTS)).astype(F32)
    nblk_pad = -(-nblk // SUBLANES) * SUBLANES
    return pl.pallas_call(
        functools.partial(_route_kernel, tm_e),
        out_shape=(jax.ShapeDtypeStruct((n, LANES), I32),
                   jax.ShapeDtypeStruct((nblk_pad, LANES), I32),
                   jax.ShapeDtypeStruct((SUBLANES, LANES), I32)),
        grid=(2, nt),
        in_specs=[pl.BlockSpec((tm, LANES), lambda ph, i: (i, 0)),
                  pl.BlockSpec((tm, tm), lambda ph, i: (0, 0)),
                  pl.BlockSpec((LANES, LANES), lambda ph, i: (0, 0))],
        out_specs=(pl.BlockSpec((tm, LANES), lambda ph, i: (i * ph, 0)),
                   pl.BlockSpec((nblk_pad, LANES), lambda ph, i: (0, 0)),
                   pl.BlockSpec((SUBLANES, LANES), lambda ph, i: (0, 0))),
        scratch_shapes=[pltpu.VMEM((nt, tm, LANES), F32), pltpu.VMEM((SUBLANES, LANES), F32),
                        pltpu.VMEM((SUBLANES, LANES), F32)],
        compiler_params=_params(("arbitrary", "arbitrary")),
        name="route",
    )(idx_all, ltri, utri)


def _row_copy_loops(tm, make_copy):
    def start(t, carry):
        for k in range(TOP_K):
            make_copy(t, k).start()
        return carry

    def wait(t, carry):
        for k in range(TOP_K):
            make_copy(t, k).wait()
        return carry

    lax.fori_loop(0, tm, start, 0)
    lax.fori_loop(0, tm, wait, 0)


def _dispatch_kernel(dest_ref, x_ref, xs_in_hbm, xs_hbm, sem):
    del xs_in_hbm
    tm = x_ref.shape[0] // SUBLANES

    def make_copy(t, k):
        d = dest_ref[0, 0, t * TOP_K + k]
        return pltpu.make_async_copy(x_ref.at[_row_tile(t)], xs_hbm.at[_row_tile(d)], sem)

    _row_copy_loops(tm, make_copy)


def _dispatch(dest3, x_tiled, xs):
    nt, _, width = dest3.shape
    return pl.pallas_call(
        _dispatch_kernel,
        out_shape=jax.ShapeDtypeStruct(xs.shape, xs.dtype),
        grid=(nt,),
        in_specs=[pl.BlockSpec((1, 1, width), lambda i: (i, 0, 0), memory_space=pltpu.SMEM),
                  pl.BlockSpec((width // TOP_K * SUBLANES, LANES), lambda i: (i, 0)),
                  pl.BlockSpec(memory_space=pl.ANY)],
        out_specs=pl.BlockSpec(memory_space=pl.ANY),
        scratch_shapes=[pltpu.SemaphoreType.DMA(())],
        input_output_aliases={2: 0},
        compiler_params=_params(("arbitrary",)),
        name="dispatch",
    )(dest3, x_tiled, xs)


def _experts_kernel(be_ref, nused_ref, x_ref, wi_ref, bi_ref, wo_ref, bo_ref, o_ref,
                    wi_scr, wo_scr):
    i = pl.program_id(0)
    active = i < nused_ref[0]
    fresh = jnp.logical_or(i == 0, be_ref[i] != be_ref[jnp.maximum(i - 1, 0)])

    @pl.when(jnp.logical_and(active, fresh))
    def _():
        wi_scr[...] = wi_ref[0].astype(BF16)
        wo_scr[...] = wo_ref[0].astype(BF16)

    @pl.when(active)
    def _():
        rows = x_ref.shape[0] // SUBLANES
        x = _load_row_tiles(x_ref, rows)
        hdn = _dot(x.astype(BF16), wi_scr[...]) + bi_ref[0]
        gate = jnp.minimum(hdn[:, :D_FF], SWIGLU_LIMIT)
        up = jnp.clip(hdn[:, D_FF:], -SWIGLU_LIMIT, SWIGLU_LIMIT)
        act = (up + 1.0) * gate * _sigmoid(SWIGLU_ALPHA * gate)
        _store_row_tiles(o_ref, _dot(act.astype(BF16), wo_scr[...]) + bo_ref[0])

    @pl.when(jnp.logical_not(active))
    def _():
        o_ref[...] = jnp.zeros_like(o_ref)


def _experts(block_e, nused, xs, w_exp_in, b_exp_in, w_exp_out, b_exp_out, tm_e):
    n_rows = xs.shape[0] // SUBLANES
    tiled = pl.BlockSpec((tm_e * SUBLANES, LANES), lambda i, be, nu: (i, 0))
    grid_spec = pltpu.PrefetchScalarGridSpec(
        num_scalar_prefetch=2,
        grid=(n_rows // tm_e,),
        in_specs=[
            tiled,
            pl.BlockSpec((1, D_MODEL, 2 * D_FF), lambda i, be, nu: (be[i], 0, 0)),
            pl.BlockSpec((1, 1, 2 * D_FF), lambda i, be, nu: (be[i], 0, 0)),
            pl.BlockSpec((1, D_FF, D_MODEL), lambda i, be, nu: (be[i], 0, 0)),
            pl.BlockSpec((1, 1, D_MODEL), lambda i, be, nu: (be[i], 0, 0)),
        ],
        out_specs=tiled,
        scratch_shapes=[pltpu.VMEM((D_MODEL, 2 * D_FF), BF16), pltpu.VMEM((D_FF, D_MODEL), BF16)],
    )
    return pl.pallas_call(
        _experts_kernel,
        out_shape=jax.ShapeDtypeStruct(xs.shape, F32),
        grid_spec=grid_spec,
        compiler_params=_params(("arbitrary",)),
        name="experts",
    )(block_e, nused, xs, w_exp_in, b_exp_in, w_exp_out, b_exp_out)


def _final_kernel(dest_ref, h1_ref, gate_ref, p_ref, wpi_ref, wpg_ref, ys_hbm, o_ref,
                  rows_scr, sem):
    tm = h1_ref.shape[0]

    def make_copy(t, k):
        d = dest_ref[0, 0, t * TOP_K + k]
        return pltpu.make_async_copy(ys_hbm.at[_row_tile(d)], rows_scr.at[k, _row_tile(t)], sem)

    _row_copy_loops(tm, make_copy)
    gates = gate_ref[...]
    y = functools.reduce(jnp.add, [gates[:, k:k + 1] * _load_row_tiles(rows_scr, tm, (k,))
                                   for k in range(TOP_K)])
    h2 = h1_ref[...] + y
    ms = jnp.mean(h2 * h2, axis=-1, keepdims=True)
    hn = (h2 * lax.rsqrt(ms + EPS)).astype(BF16)
    gate = _sigmoid(_dot(hn, wpg_ref[...]))
    o_ref[...] = h2 + gate * _dot(p_ref[...].astype(BF16), wpi_ref[...])


def _final(dest3, h1, gates, p2d, ys, prm):
    nt, _, width = dest3.shape
    tm = width // TOP_K
    row = lambda i: (i, 0)
    const = lambda i: (0, 0)
    return pl.pallas_call(
        _final_kernel,
        out_shape=jax.ShapeDtypeStruct(h1.shape, F32),
        grid=(nt,),
        in_specs=[pl.BlockSpec((1, 1, width), lambda i: (i, 0, 0), memory_space=pltpu.SMEM),
                  pl.BlockSpec((tm, D_MODEL), row),
                  pl.BlockSpec((tm, LANES), row),
                  pl.BlockSpec((tm, PLE_DIM), row),
                  pl.BlockSpec((PLE_DIM, D_MODEL), const),
                  pl.BlockSpec((D_MODEL, D_MODEL), const),
                  pl.BlockSpec(memory_space=pl.ANY)],
        out_specs=pl.BlockSpec((tm, D_MODEL), row),
        scratch_shapes=[pltpu.VMEM((TOP_K, tm * SUBLANES, LANES), F32), pltpu.SemaphoreType.DMA(())],
        compiler_params=_params(("arbitrary",)),
        name="final",
    )(dest3, h1, gates, p2d, prm["w_ple_in"], prm["w_ple_gate"], ys)


TOKEN_TILE = 256
PROJ_ROW_TILE = 1024
GLA_BLOCK = 256
EXPERT_TILE = 256
SAMPLE_PAD = SUBLANES


def kernel(x_prompt, x_sample, cache_da1_kv, cache_da2_kv, cache_da3_kv, state_gla, p_prompt, p_sample, norm_mix_g, w_in, w_gla_alpha, b_gla_alpha, gla_norm_g, w_gla_out, qk_norm_q, qk_norm_k, w_da_out, w_o, norm_moe_g, w_router, b_router, w_exp_in, b_exp_in, w_exp_out, b_exp_out, w_ple_in, w_ple_gate):
    depth = w_in.shape[0]
    assert depth == 1, "single-layer trunk"
    l = 0
    prm = _prep_params(norm_mix_g[l], w_in[l], w_gla_alpha[l], b_gla_alpha[l], gla_norm_g[l],
                       w_gla_out[l], qk_norm_q[l], qk_norm_k[l], w_da_out[l], w_o[l], norm_moe_g[l],
                       w_router[l], b_router[l], b_exp_in[l], b_exp_out[l], w_ple_in[l], w_ple_gate[l])
    n_b, seq, _ = x_prompt.shape
    d_b, t_new, _ = x_sample.shape
    caches = (cache_da1_kv[l], cache_da2_kv[l], cache_da3_kv[l])
    assert seq % (DA_SPAN * DA_DILATIONS[-1]) == 0 and seq % PROJ_ROW_TILE == 0
    assert t_new <= SAMPLE_PAD and (d_b * SAMPLE_PAD) % TOKEN_TILE == 0
    for g in range(N_DA_GROUPS):
        assert caches[g].shape[1] == DA_WINDOWS[g]

    hp = x_prompt.reshape(n_b * seq, D_MODEL)
    pp, la_p = _proj(hp, prm, PROJ_ROW_TILE, dilate=True)
    zero_state = jnp.zeros((n_b, GLA_HEADS, GLA_DK, GLA_DV), F32)
    og_p, state_p = _gla(pp, la_p, zero_state, prm["gla_norm_g"], n_b, seq, GLA_BLOCK, GLA_CHUNK)
    da_p = [_da_prompt(pp, g, n_b, seq, PROJ_ROW_TILE) for g in range(N_DA_GROUPS)]
    h1_p, xn_p, idx_p, gate_p = _mix(pp, og_p, [o for o, _ in da_p], [s for _, s in da_p], hp, prm,
                                     TOKEN_TILE, PROJ_ROW_TILE)

    pad_t = ((0, 0), (0, SAMPLE_PAD - t_new), (0, 0))
    hs = jnp.pad(x_sample, pad_t).reshape(d_b * SAMPLE_PAD, D_MODEL)
    ps, la_s = _proj(hs, prm, TOKEN_TILE, dilate=False)
    og_s, state_s = _gla(ps, la_s, state_gla[l], prm["gla_norm_g"], d_b, SAMPLE_PAD, SAMPLE_PAD,
                         SAMPLE_PAD, t_valid=t_new)
    caches_t = [jnp.transpose(c, (0, 2, 3, 4, 1)) for c in caches]
    o_s, *new_caches_t = _da_sample(ps, caches_t, d_b, SAMPLE_PAD, t_new)
    h1_s, xn_s, idx_s, gate_s = _mix(ps, og_s, [o_s], [], hs, prm, TOKEN_TILE)

    n_p, n_s = hp.shape[0], hs.shape[0]
    n_rows = (n_p + n_s) * TOP_K + N_EXPERTS * EXPERT_TILE
    dest, block_e, nused = _route(jnp.concatenate([idx_p, idx_s], axis=0), TOKEN_TILE, EXPERT_TILE,
                                  n_rows // EXPERT_TILE)
    dest3 = dest[:, :TOP_K].reshape(-1, 1, TOKEN_TILE * TOP_K)
    dest3_p, dest3_s = dest3[:n_p // TOKEN_TILE], dest3[n_p // TOKEN_TILE:]
    xs = jnp.zeros((n_rows * SUBLANES, LANES), F32)
    xs = _dispatch(dest3_p, xn_p, xs)
    xs = _dispatch(dest3_s, xn_s, xs)
    ys = _experts(block_e[:n_rows // EXPERT_TILE, 0], nused[0, :1], xs, w_exp_in[l], prm["b_exp_in"],
                  w_exp_out[l], prm["b_exp_out"], EXPERT_TILE)
    out_p = _final(dest3_p, h1_p, gate_p, p_prompt[l].reshape(n_p, PLE_DIM), ys, prm)
    out_s = _final(dest3_s, h1_s, gate_s, jnp.pad(p_sample[l], pad_t).reshape(n_s, PLE_DIM), ys, prm)

    y_prompt = out_p.reshape(n_b, seq, D_MODEL)
    y_sample = out_s.reshape(d_b, SAMPLE_PAD, D_MODEL)[:, :t_new]
    outs_p, outs_s = [], []
    for g in range(N_DA_GROUPS):
        win, dil = min(DA_WINDOWS[g], seq), DA_DILATIONS[g]
        ck = (COL_DA + 3 * g + 1) * PROJ_TN
        n_tiles = -(-win // PROJ_ROW_TILE)
        kv_p = pp.reshape(n_b, seq // PROJ_ROW_TILE, dil, PROJ_ROW_TILE // dil, PROJ_WIDTH)
        kv_p = kv_p[:, -n_tiles:, :, :, ck:ck + 2 * DA_WIDTH]
        kv_p = jnp.transpose(kv_p, (0, 1, 3, 2, 4)).reshape(n_b, n_tiles * PROJ_ROW_TILE, -1)
        outs_p.append(kv_p[:, -win:].reshape(1, n_b, win, 2, DA_HEADS, DA_HEAD_DIM))
        outs_s.append(jnp.transpose(new_caches_t[g], (0, 4, 1, 2, 3))[None])
    return (y_prompt, y_sample, state_p[None], state_s[None], outs_p[0], outs_s[0], outs_p[1],
            outs_s[1], outs_p[2], outs_s[2])
```

```python
import functools

import jax
import jax.numpy as jnp
from jax import lax
from jax.experimental import pallas as pl
from jax.experimental.pallas import tpu as pltpu

F32 = jnp.float32
BF16 = jnp.bfloat16
I32 = jnp.int32

D_MODEL = 1024
GLA_HEADS = 4
GLA_DK = 128
GLA_DV = 256
GLA_QK = GLA_HEADS * GLA_DK
GLA_V = GLA_HEADS * GLA_DV
GLA_RANK = 16
GLA_TAU = 16.0
GLA_CHUNK = 32
DA_WINDOWS = (128, 512, 2048)
DA_DILATIONS = (1, 4, 16)
DA_SPAN = 128
N_DA_GROUPS = 3
DA_HEADS = 8
DA_HEAD_DIM = 64
DA_WIDTH = DA_HEADS * DA_HEAD_DIM
N_EXPERTS = 32
TOP_K = 4
D_FF = D_MODEL
SWIGLU_LIMIT = 7.0
SWIGLU_ALPHA = 1.702
PLE_DIM = 256
EPS = 1e-6

LANES = 128
SUBLANES = 8
VMEM_LIMIT = 56 * 1024 * 1024

PROJ_TN = 512
COL_GQ, COL_GK, COL_GV, COL_GR, COL_GA, COL_GB, COL_DA = 0, 1, 2, 4, 6, 8, 10
PROJ_TILES = COL_DA + 3 * N_DA_GROUPS
PROJ_WIDTH = PROJ_TILES * PROJ_TN
NEG_BIG = -1e30

HIGHEST = lax.Precision.HIGHEST


def _dot(a, b):
    return jnp.dot(a, b, preferred_element_type=F32)


def _dot_nt(a, b):
    return lax.dot_general(a, b, (((1,), (1,)), ((), ())), preferred_element_type=F32)


def _dot_tn(a, b):
    return lax.dot_general(a, b, (((0,), (0,)), ((), ())), preferred_element_type=F32)


def _sigmoid(x):
    return 1.0 / (1.0 + jnp.exp(-x))


def _params(sem, vmem=VMEM_LIMIT):
    return pltpu.CompilerParams(dimension_semantics=sem, vmem_limit_bytes=vmem)


assert D_MODEL == SUBLANES * LANES


def _row_tile(r):
    return pl.ds(pl.multiple_of(r * SUBLANES, SUBLANES), SUBLANES)


def _store_row_tiles(ref, value, lead=()):
    rows = value.shape[0]
    for s in range(SUBLANES):
        ref[(*lead, pl.ds(s, rows, stride=SUBLANES), slice(None))] = value[:, s * LANES:(s + 1) * LANES]


def _load_row_tiles(ref, rows, lead=()):
    return jnp.concatenate(
        [ref[(*lead, pl.ds(s, rows, stride=SUBLANES), slice(None))] for s in range(SUBLANES)], axis=1)


def _proj_kernel(dilate, x_ref, g_ref, w_ref, wlr_ref, wal_ref, bal_ref, qkg_ref, bd_ref,
                 p_ref, la_ref, xn_scr, stage_scr):
    j = pl.program_id(1)
    tm = x_ref.shape[0]

    @pl.when(j == 0)
    def _():
        x = x_ref[...]
        ms = jnp.mean(x * x, axis=-1, keepdims=True)
        xn = (x * lax.rsqrt(ms + EPS) * g_ref[...]).astype(BF16)
        xn_scr[...] = xn
        glr = _dot(xn, wlr_ref[...])
        z = _dot(glr.astype(BF16), wal_ref[...]) + bal_ref[...]
        log_sig = jnp.minimum(z, 0.0) - jnp.log(1.0 + jnp.exp(-jnp.abs(z)))
        la_ref[...] = log_sig * (1.0 / GLA_TAU)

    acc = _dot(xn_scr[...], w_ref[...])
    is_qk = jnp.logical_and(j >= COL_DA, (j - COL_DA) % 3 != 2)
    group = (j - COL_DA) // 3
    staged = [jnp.logical_and(j >= COL_DA, group == g) if dilate and DA_DILATIONS[g] > 1 else None
              for g in range(N_DA_GROUPS)]
    any_staged = functools.reduce(jnp.logical_or, [s for s in staged if s is not None], False)

    def qk_normed():
        ssum = _dot((acc * acc).astype(BF16), bd_ref[...])
        return acc * lax.rsqrt(ssum * (1.0 / DA_HEAD_DIM) + EPS) * qkg_ref[...]

    def emit(value):
        if any_staged is False:
            p_ref[...] = value.astype(p_ref.dtype)
            return

        @pl.when(any_staged)
        def _():
            for c in range(PROJ_TN // LANES):
                stage_scr[c] = value[:, c * LANES:(c + 1) * LANES]

        @pl.when(jnp.logical_not(any_staged))
        def _():
            p_ref[...] = value.astype(p_ref.dtype)

    @pl.when(is_qk)
    def _():
        emit(qk_normed())

    @pl.when(jnp.logical_not(is_qk))
    def _():
        emit(acc)

    for g in range(N_DA_GROUPS):
        if staged[g] is None:
            continue
        dil = DA_DILATIONS[g]
        per = tm // dil

        @pl.when(staged[g])
        def _(dil=dil, per=per):
            for r in range(dil):
                for c in range(PROJ_TN // LANES):
                    p_ref[r * per:(r + 1) * per, c * LANES:(c + 1) * LANES] = (
                        stage_scr[c, pl.ds(r, per, stride=dil), :].astype(p_ref.dtype))


def _proj(x2d, prm, tm, dilate, p_dtype):
    m = x2d.shape[0]
    grid = (m // tm, PROJ_TILES)
    const = lambda i, j: (0, 0)
    return pl.pallas_call(
        functools.partial(_proj_kernel, dilate),
        out_shape=(jax.ShapeDtypeStruct((m, PROJ_WIDTH), p_dtype),
                   jax.ShapeDtypeStruct((m, GLA_QK), F32)),
        grid=grid,
        in_specs=[
            pl.BlockSpec((tm, D_MODEL), lambda i, j: (i, 0)),
            pl.BlockSpec((1, D_MODEL), const),
            pl.BlockSpec((D_MODEL, PROJ_TN), lambda i, j: (0, j)),
            pl.BlockSpec((D_MODEL, LANES), const),
            pl.BlockSpec((LANES, GLA_QK), const),
            pl.BlockSpec((1, GLA_QK), const),
            pl.BlockSpec((1, PROJ_TN), lambda i, j: (0, jnp.maximum(j - COL_DA, 0))),
            pl.BlockSpec((PROJ_TN, PROJ_TN), const),
        ],
        out_specs=(pl.BlockSpec((tm, PROJ_TN), lambda i, j: (i, j)),
                   pl.BlockSpec((tm, GLA_QK), lambda i, j: (i, 0))),
        scratch_shapes=[pltpu.VMEM((tm, D_MODEL), BF16),
                        pltpu.VMEM((PROJ_TN // LANES, tm, LANES), F32)],
        compiler_params=_params(("arbitrary", "arbitrary")),
        name="proj",
    )(x2d, prm["norm_mix_g"], prm["w_main"], prm["w_lr"], prm["w_alpha"], prm["b_alpha"],
      prm["qk_gain"], prm["head_ones"])


def _gla_kernel(chunk, t_valid, q_ref, k_ref, v_ref, r_ref, la_ref, s0_ref, g_ref, tri_ref,
                o_ref, s_ref, s_scr):
    t = pl.program_id(1)
    tb = q_ref.shape[0]
    n_chunks = tb // chunk
    row_ok = lax.broadcasted_iota(I32, (tb, GLA_DK), 0) < t_valid

    @pl.when(t == 0)
    def _():
        for h in range(GLA_HEADS):
            s_scr[h] = s0_ref[0, h].T

    tri = tri_ref[...]
    causal = tri > 0
    for h in range(GLA_HEADS):
        ks = slice(h * GLA_DK, (h + 1) * GLA_DK)
        vs = slice(h * GLA_DV, (h + 1) * GLA_DV)
        la = la_ref[:, ks]
        if t_valid < tb:
            la = jnp.where(row_ok, la, 0.0)
        la_hi = la.astype(BF16)
        la_lo = (la - la_hi.astype(F32)).astype(BF16)
        cum = _dot(tri, la_hi) + _dot(tri, la_lo)
        q = q_ref[:, ks].astype(F32) * (GLA_DK ** -0.5)
        k = k_ref[:, ks].astype(F32)
        v = v_ref[:, vs].astype(BF16)
        q_dec = (q * jnp.exp(cum)).astype(BF16)
        k_inv = (k * jnp.exp(-cum)).astype(BF16)
        att = jnp.where(causal, _dot_nt(q_dec, k_inv), 0.0)
        o_intra = _dot(att.astype(BF16), v)
        s_t = s_scr[h]
        for c in range(n_chunks):
            rows = slice(c * chunk, (c + 1) * chunk)
            tot = cum[(c + 1) * chunk - 1:(c + 1) * chunk, :]
            k_end = (k[rows] * jnp.exp(tot - cum[rows])).astype(BF16)
            o_c = o_intra[rows] + _dot_nt(q_dec[rows], s_t.astype(BF16))
            s_t = s_t * jnp.exp(tot) + _dot_tn(v[rows], k_end)
            ms = jnp.mean(o_c * o_c, axis=-1, keepdims=True)
            gate = r_ref[rows, vs].astype(F32)
            o_ref[rows, vs] = o_c * lax.rsqrt(ms + EPS) * g_ref[...] * (gate * _sigmoid(gate))
        s_scr[h] = s_t

    @pl.when(t == pl.num_programs(1) - 1)
    def _():
        for h in range(GLA_HEADS):
            s_ref[0, h] = s_scr[h].T


def _gla(p, la, s0, gla_norm_g, n_batch, t_len, tb, chunk, t_valid=None):
    nt = t_len // tb
    t_valid = tb if t_valid is None else t_valid
    r = jnp.arange(tb)
    same = (r[:, None] // chunk) == (r[None, :] // chunk)
    tri = (same & (r[:, None] >= r[None, :])).astype(BF16)
    row = lambda b, t: b * nt + t
    const = lambda b, t: (0, 0)
    return pl.pallas_call(
        functools.partial(_gla_kernel, chunk, t_valid),
        out_shape=(jax.ShapeDtypeStruct((n_batch * t_len, GLA_V), F32),
                   jax.ShapeDtypeStruct((n_batch, GLA_HEADS, GLA_DK, GLA_DV), F32)),
        grid=(n_batch, nt),
        in_specs=[
            pl.BlockSpec((tb, GLA_QK), lambda b, t: (row(b, t), COL_GQ)),
            pl.BlockSpec((tb, GLA_QK), lambda b, t: (row(b, t), COL_GK)),
            pl.BlockSpec((tb, GLA_V), lambda b, t: (row(b, t), COL_GV * PROJ_TN // GLA_V)),
            pl.BlockSpec((tb, GLA_V), lambda b, t: (row(b, t), COL_GR * PROJ_TN // GLA_V)),
            pl.BlockSpec((tb, GLA_QK), lambda b, t: (row(b, t), 0)),
            pl.BlockSpec((1, GLA_HEADS, GLA_DK, GLA_DV), lambda b, t: (b, 0, 0, 0)),
            pl.BlockSpec((1, GLA_DV), const),
            pl.BlockSpec((tb, tb), const),
        ],
        out_specs=(pl.BlockSpec((tb, GLA_V), lambda b, t: (row(b, t), 0)),
                   pl.BlockSpec((1, GLA_HEADS, GLA_DK, GLA_DV), lambda b, t: (b, 0, 0, 0))),
        scratch_shapes=[pltpu.VMEM((GLA_HEADS, GLA_DV, GLA_DK), F32)],
        compiler_params=_params(("arbitrary", "arbitrary")),
        name="gla",
    )(p, p, p, p, la, s0, gla_norm_g, tri)


def _prep_params(norm_mix_g, w_in, w_gla_alpha, b_gla_alpha, gla_norm_g, w_gla_out, qk_norm_q,
                 qk_norm_k, w_da_out, w_o, norm_moe_g, w_router, b_router, b_exp_in, b_exp_out,
                 w_ple_in, w_ple_gate):
    o_lr = GLA_QK * 2 + GLA_V * 2
    o_da = o_lr + GLA_RANK
    o_ga = o_da + 3 * N_DA_GROUPS * DA_WIDTH
    w_main = jnp.concatenate([w_in[:, :o_lr], w_in[:, o_ga:], w_in[:, o_da:o_ga]], axis=1).astype(BF16)
    w_lr = jnp.pad(w_in[:, o_lr:o_da], ((0, 0), (0, LANES - GLA_RANK))).astype(BF16)
    w_alpha = jnp.pad(w_gla_alpha, ((0, LANES - GLA_RANK), (0, 0))).astype(BF16)
    gains = []
    for g in range(N_DA_GROUPS):
        gains.append(jnp.tile(qk_norm_q[g], DA_HEADS) * (DA_HEAD_DIM ** -0.5))
        gains.append(jnp.tile(qk_norm_k[g], DA_HEADS))
        gains.append(jnp.ones((DA_WIDTH,), F32))
    qk_gain = jnp.concatenate(gains)[None, :]
    c = jnp.arange(PROJ_TN)
    head_ones = (c[:, None] // DA_HEAD_DIM == c[None, :] // DA_HEAD_DIM).astype(BF16)
    pad_e = LANES - N_EXPERTS
    w_r = jnp.pad(w_router, ((0, 0), (0, pad_e)))
    w_r_hi = w_r.astype(BF16)
    w_r_split = jnp.stack([w_r_hi, (w_r - w_r_hi.astype(F32)).astype(BF16)])
    return dict(
        norm_mix_g=norm_mix_g[None, :], w_main=w_main, w_lr=w_lr, w_alpha=w_alpha,
        b_alpha=b_gla_alpha[None, :], qk_gain=qk_gain, head_ones=head_ones,
        gla_norm_g=gla_norm_g[None, :], w_gla_out=w_gla_out.astype(BF16),
        w_da_out=w_da_out.astype(BF16), w_o=w_o.astype(BF16), norm_moe_g=norm_moe_g[None, :],
        w_router=w_r_split,
        b_router=jnp.pad(b_router, (0, pad_e), constant_values=NEG_BIG)[None, :],
        b_exp_in=b_exp_in[:, None, :], b_exp_out=b_exp_out[:, None, :],
        w_ple_in=w_ple_in.astype(BF16), w_ple_gate=w_ple_gate.astype(BF16),
    )


def _da_prompt_kernel(q_ref, kc_ref, kp_ref, vc_ref, vp_ref, bias_ref, o_ref, lse_ref):
    bias = bias_ref[0]
    low = lax.broadcasted_iota(I32, (DA_SPAN, LANES), 1) < DA_HEAD_DIM
    pieces = q_ref.shape[1:4:2]

    def rows(ref, cs):
        return ref[0, :, 0, :, cs].reshape(DA_SPAN, LANES)

    for a in range(DA_WIDTH // LANES):
        cs = slice(a * LANES, (a + 1) * LANES)
        q2 = rows(q_ref, cs)
        k2 = jnp.concatenate([rows(kp_ref, cs), rows(kc_ref, cs)], axis=0).astype(BF16)
        v2 = jnp.concatenate([rows(vp_ref, cs), rows(vc_ref, cs)], axis=0).astype(BF16)
        res = []
        for sel in (low, jnp.logical_not(low)):
            s = _dot_nt(jnp.where(sel, q2, 0.0).astype(BF16), k2) + bias
            m = jnp.max(s, axis=-1, keepdims=True)
            p = jnp.exp(s - m)
            l = jnp.sum(p, axis=-1, keepdims=True)
            res.append((_dot(p.astype(BF16), v2) / l, m + jnp.log(l)))
        o_ref[0, :, 0, :, cs] = jnp.where(low, res[0][0], res[1][0]).reshape(*pieces, LANES)
        lse_ref[0, :, 0, :, cs] = jnp.where(low, res[0][1], res[1][1]).reshape(*pieces, LANES)


def _da_bias():
    qi = jnp.arange(DA_SPAN)[:, None]
    kj = jnp.arange(2 * DA_SPAN)[None, :]
    d = kj - qi
    band = (d >= 0) & (d <= DA_SPAN)
    first = band & (kj >= DA_SPAN)
    return jnp.where(jnp.stack([first, band]), 0.0, NEG_BIG).astype(F32)


def _da_prompt(p, g, n_batch, seq, tile):
    dil = DA_DILATIONS[g]
    per = tile // dil
    nb = seq // dil // DA_SPAN
    p5 = p.reshape(n_batch, seq // tile, dil, per, PROJ_WIDTH)
    cq, ck, cv = (COL_DA + 3 * g + i for i in range(3))
    if per >= DA_SPAN:
        assert per % DA_SPAN == 0
        blk = (1, 1, 1, DA_SPAN, DA_WIDTH)
        place = lambda n: (n // (per // DA_SPAN), n % (per // DA_SPAN))
    else:
        assert DA_SPAN % per == 0 and per % SUBLANES == 0
        blk = (1, DA_SPAN // per, 1, per, DA_WIDTH)
        place = lambda n: (n, 0)

    def spec(c, back):
        def index(b, r, n):
            tile_i, row_i = place(jnp.maximum(n - back, 0))
            return (b, tile_i, r, row_i, c)
        return pl.BlockSpec(blk, index)

    out_sds = jax.ShapeDtypeStruct((n_batch, seq // tile, dil, per, DA_WIDTH), F32)
    return pl.pallas_call(
        _da_prompt_kernel,
        out_shape=(out_sds, out_sds),
        grid=(n_batch, dil, nb),
        in_specs=[spec(cq, 0), spec(ck, 0), spec(ck, 1), spec(cv, 0), spec(cv, 1),
                  pl.BlockSpec((1, DA_SPAN, 2 * DA_SPAN), lambda b, r, n: (jnp.minimum(n, 1), 0, 0))],
        out_specs=(spec(0, 0), spec(0, 0)),
        compiler_params=_params(("arbitrary", "arbitrary", "arbitrary")),
        name=f"da_prompt{g}",
    )(p5, p5, p5, p5, p5, _da_bias())


def _mix_kernel(dils, *refs):
    og_ref = refs[0]
    n_groups = len(dils) if dils else 1
    n_lse = n_groups if dils else 0
    o_refs = refs[1:1 + n_groups]
    l_refs = refs[1 + n_groups:1 + n_groups + n_lse]
    (ga_ref, gb_ref, h_ref, wga_ref, wda_ref, wo_ref, gm_ref, wr_ref, br_ref,
     h1_ref, xn_ref, idx_ref, gate_ref) = refs[1 + n_groups + n_lse:14 + n_groups + n_lse]
    scratch = iter(refs[14 + n_groups + n_lse:])

    def token_major(ref, dil):
        if dil == 1:
            return ref[0, 0]
        scr = next(scratch)
        per = ref.shape[2]
        for r in range(dil):
            for c in range(DA_WIDTH // LANES):
                scr[c, pl.ds(r, per, stride=dil), :] = ref[0, r, :, c * LANES:(c + 1) * LANES]
        return jnp.concatenate([scr[c] for c in range(DA_WIDTH // LANES)], axis=1)

    if not dils:
        o_da = o_refs[0][...]
    else:
        outs = [token_major(r, d) for r, d in zip(o_refs, dils)]
        lses = [token_major(r, d) for r, d in zip(l_refs, dils)]
        m = functools.reduce(jnp.maximum, lses)
        es = [jnp.exp(l - m) for l in lses]
        num = functools.reduce(jnp.add, [e * o for e, o in zip(es, outs)])
        o_da = num / functools.reduce(jnp.add, es)
    y_a = _dot(og_ref[...].astype(BF16), wga_ref[...])
    y_b = _dot(o_da.astype(BF16), wda_ref[...])
    y = _sigmoid(ga_ref[...].astype(F32)) * y_a + _sigmoid(gb_ref[...].astype(F32)) * y_b
    h1 = h_ref[...] + _dot(y.astype(BF16), wo_ref[...])
    h1_ref[...] = h1
    ms = jnp.mean(h1 * h1, axis=-1, keepdims=True)
    xn = h1 * lax.rsqrt(ms + EPS) * gm_ref[...]
    _store_row_tiles(xn_ref, xn)
    xn_hi = xn.astype(BF16)
    xn_lo = (xn - xn_hi.astype(F32)).astype(BF16)
    logits = (_dot(xn_hi, wr_ref[0]) + _dot(xn_lo, wr_ref[0]) + _dot(xn_hi, wr_ref[1])) + br_ref[...]
    lane = lax.broadcasted_iota(I32, logits.shape, 1)
    lane_f = lane.astype(F32)
    work = logits
    idx_out = jnp.zeros(logits.shape, I32)
    val_out = jnp.zeros(logits.shape, F32)
    vals = []
    for k in range(TOP_K):
        top = jnp.max(work, axis=-1, keepdims=True)
        pick = jnp.min(jnp.where(work == top, lane_f, float(LANES)), axis=-1, keepdims=True)
        pick = pick.astype(I32)
        vals.append(top)
        idx_out = jnp.where(lane == k, pick, idx_out)
        work = jnp.where(lane == pick, -3e38, work)
    es = [jnp.exp(v - vals[0]) for v in vals]
    den = functools.reduce(jnp.add, es)
    for k in range(TOP_K):
        val_out = jnp.where(lane == k, es[k] / den, val_out)
    idx_ref[...] = idx_out
    gate_ref[...] = val_out


def _mix(p, o_gla, o_das, lses, h2d, prm, tm, tile=None):
    m = h2d.shape[0]
    row = lambda i: (i, 0)
    const = lambda i: (0, 0)
    wide = pl.BlockSpec((tm, D_MODEL), row)
    small = pl.BlockSpec((tm, LANES), row)
    if lses:
        dils = DA_DILATIONS
        sub = tile // tm
        da_in, da_specs, scratch = [], [], []
        for arr, dil in list(zip(o_das, dils)) + list(zip(lses, dils)):
            assert (tm // dil) % SUBLANES == 0
            da_in.append(arr.reshape(-1, dil, tile // dil, DA_WIDTH))
            da_specs.append(pl.BlockSpec((1, dil, tm // dil, DA_WIDTH),
                                         lambda i: (i // sub, 0, i % sub, 0)))
            if dil > 1:
                scratch.append(pltpu.VMEM((DA_WIDTH // LANES, tm, LANES), F32))
    else:
        dils, da_in, scratch = None, list(o_das), []
        da_specs = [pl.BlockSpec((tm, DA_WIDTH), row)]
    return pl.pallas_call(
        functools.partial(_mix_kernel, dils),
        out_shape=(jax.ShapeDtypeStruct((m, D_MODEL), F32),
                   jax.ShapeDtypeStruct((m * SUBLANES, LANES), F32),
                   jax.ShapeDtypeStruct((m, LANES), I32), jax.ShapeDtypeStruct((m, LANES), F32)),
        grid=(m // tm,),
        scratch_shapes=scratch,
        in_specs=[wide] + da_specs + [
            pl.BlockSpec((tm, D_MODEL), lambda i: (i, COL_GA * PROJ_TN // D_MODEL)),
            pl.BlockSpec((tm, D_MODEL), lambda i: (i, COL_GB * PROJ_TN // D_MODEL)),
            wide,
            pl.BlockSpec((GLA_V, D_MODEL), const),
            pl.BlockSpec((DA_WIDTH, D_MODEL), const),
            pl.BlockSpec((D_MODEL, D_MODEL), const),
            pl.BlockSpec((1, D_MODEL), const),
            pl.BlockSpec((2, D_MODEL, LANES), lambda i: (0, 0, 0)),
            pl.BlockSpec((1, LANES), const),
        ],
        out_specs=(wide, pl.BlockSpec((tm * SUBLANES, LANES), row), small, small),
        compiler_params=_params(("arbitrary",)),
        name="mix",
    )(o_gla, *da_in, p, p, h2d, prm["w_gla_out"], prm["w_da_out"], prm["w_o"],
      prm["norm_moe_g"], prm["w_router"], prm["b_router"])


def _da_sample_kernel(t_new, *refs):
    qkv = [refs[3 * g:3 * g + 3] for g in range(N_DA_GROUPS)]
    c_refs = refs[9:12]
    sel_ref = refs[12]
    o_ref = refs[13]
    new_refs = refs[14:17]
    t_pad = o_ref.shape[0]
    q_row = lax.broadcasted_iota(I32, (t_pad, t_pad), 0)
    k_row = lax.broadcasted_iota(I32, (t_pad, t_pad), 1)
    lane = lax.broadcasted_iota(I32, (DA_HEAD_DIM, LANES), 1)
    heads = []
    for hh in range(LANES // DA_HEAD_DIM):
        hs = slice(hh * DA_HEAD_DIM, (hh + 1) * DA_HEAD_DIM)
        lses, accs, dens = [], [], []
        for g in range(N_DA_GROUPS):
            dil, win = DA_DILATIONS[g], DA_WINDOWS[g]
            q_ref, k_ref, v_ref = qkv[g]
            q = q_ref[:, hs].astype(BF16)
            k_new = k_ref[:, hs]
            v_new = v_ref[:, hs]
            kt = c_refs[g][0, 0, hh]
            vt = c_refs[g][0, 1, hh]
            t_idx = lax.broadcasted_iota(I32, (t_pad, win), 0)
            off = lax.broadcasted_iota(I32, (t_pad, win), 1) - t_idx
            ok = jnp.logical_and(off >= 0, jnp.bitwise_and(off, dil - 1) == 0)
            s = jnp.where(ok, _dot(q, kt.astype(BF16)), NEG_BIG)
            back = q_row - k_row
            ok_new = jnp.logical_and(jnp.logical_and(back >= 0, jnp.bitwise_and(back, dil - 1) == 0),
                                     k_row < t_new)
            s_new = jnp.where(ok_new, _dot_nt(q, k_new.astype(BF16)), NEG_BIG)
            m = jnp.maximum(jnp.max(s, axis=-1, keepdims=True), jnp.max(s_new, axis=-1, keepdims=True))
            p = jnp.exp(s - m)
            p_new = jnp.exp(s_new - m)
            dens.append(jnp.sum(p, axis=-1, keepdims=True) + jnp.sum(p_new, axis=-1, keepdims=True))
            accs.append(_dot_nt(p.astype(BF16), vt.astype(BF16))
                        + _dot(p_new.astype(BF16), v_new.astype(BF16)))
            lses.append(m + jnp.log(dens[-1]))
            for kv, (old, new) in enumerate(((kt, k_new), (vt, v_new))):
                moved = pltpu.roll(old, win - t_new, axis=1)
                tail = lax.dot_general(new, sel_ref[...], (((0,), (0,)), ((), ())),
                                       preferred_element_type=F32, precision=HIGHEST)
                tail = jnp.where(lane >= LANES - t_new, tail, moved[:, win - LANES:])
                if win > LANES:
                    new_refs[g][0, kv, hh, :, :win - LANES] = moved[:, :win - LANES]
                new_refs[g][0, kv, hh, :, win - LANES:] = tail
        top = functools.reduce(jnp.maximum, lses)
        ws = [jnp.exp(l - top) for l in lses]
        wsum = functools.reduce(jnp.add, ws)
        heads.append(functools.reduce(
            jnp.add, [accs[g] * (ws[g] / (wsum * dens[g])) for g in range(N_DA_GROUPS)]))
    o_ref[...] = jnp.concatenate(heads, axis=1)


def _da_sample(p_s, caches_t, n_batch, t_pad, t_new):
    pair = LANES // DA_HEAD_DIM
    t = jnp.arange(t_pad)[:, None]
    sel = ((jnp.arange(LANES)[None, :] == LANES - t_new + t) & (t < t_new)).astype(F32)
    per_tile = PROJ_TN // LANES
    qkv_specs = [pl.BlockSpec((t_pad, LANES), lambda b, a, col=COL_DA + i: (b, col * per_tile + a))
                 for i in range(3 * N_DA_GROUPS)]
    c_specs = [pl.BlockSpec((1, 2, pair, DA_HEAD_DIM, DA_WINDOWS[g]), lambda b, a: (b, 0, a, 0, 0))
               for g in range(N_DA_GROUPS)]
    return pl.pallas_call(
        functools.partial(_da_sample_kernel, t_new),
        out_shape=(jax.ShapeDtypeStruct((n_batch * t_pad, DA_WIDTH), F32),
                   *[jax.ShapeDtypeStruct(c.shape, F32) for c in caches_t]),
        grid=(n_batch, DA_HEADS // pair),
        in_specs=qkv_specs + c_specs + [pl.BlockSpec((t_pad, LANES), lambda b, a: (0, 0))],
        out_specs=(pl.BlockSpec((t_pad, LANES), lambda b, a: (b, a)), *c_specs),
        compiler_params=_params(("arbitrary", "arbitrary")),
        name="da_sample",
    )(*([p_s] * (3 * N_DA_GROUPS)), *caches_t, sel)


def _route_kernel(tm_e, idx_ref, ltri_ref, utri_ref, dest_ref, be_ref, nused_ref,
                  rank_scr, carry_scr, pstart_scr):
    ph = pl.program_id(0)
    i = pl.program_id(1)
    tm = idx_ref.shape[0]
    lane = lax.broadcasted_iota(I32, (tm, LANES), 1)
    idx = idx_ref[...]
    hots = [lane == idx[:, k:k + 1] for k in range(TOP_K)]

    @pl.when(ph == 0)
    def _():
        @pl.when(i == 0)
        def _():
            carry_scr[...] = jnp.zeros_like(carry_scr)

        cnt = functools.reduce(jnp.add, [h.astype(F32) for h in hots])
        before = _dot(ltri_ref[...], cnt.astype(BF16)) + carry_scr[0:1, :]
        rank = jnp.zeros((tm, LANES), F32)
        for k in range(TOP_K):
            r_k = jnp.sum(jnp.where(hots[k], before, 0.0), axis=-1, keepdims=True)
            rank = jnp.where(lane == k, r_k, rank)
        rank_scr[i] = rank
        carry_scr[...] = carry_scr[...] + jnp.sum(cnt, axis=0, keepdims=True)

    @pl.when(ph == 1)
    def _():
        @pl.when(i == 0)
        def _():
            counts = carry_scr[...]
            padded = jnp.floor((counts + (tm_e - 1)) * (1.0 / tm_e)) * tm_e
            pend = jnp.dot(padded, utri_ref[...], preferred_element_type=F32, precision=HIGHEST)
            pstart_scr[...] = pend - padded
            nblk = be_ref.shape[0]
            start = lax.broadcasted_iota(I32, (nblk, LANES), 0).astype(F32) * tm_e
            is_e = lax.broadcasted_iota(I32, (nblk, LANES), 1) < N_EXPERTS
            done = jnp.where(jnp.logical_and(pend[0:1, :] <= start, is_e), 1.0, 0.0)
            be = jnp.minimum(jnp.sum(done, axis=-1, keepdims=True), N_EXPERTS - 1.0)
            be_ref[...] = jnp.broadcast_to(be, (nblk, LANES)).astype(I32)
            total = pend[:, N_EXPERTS - 1:N_EXPERTS] * (1.0 / tm_e)
            nused_ref[...] = jnp.broadcast_to(total, nused_ref.shape).astype(I32)

        dest = rank_scr[i]
        for k in range(TOP_K):
            d_k = jnp.sum(jnp.where(hots[k], pstart_scr[0:1, :], 0.0), axis=-1, keepdims=True)
            dest = dest + jnp.where(lane == k, d_k, 0.0)
        dest_ref[...] = dest.astype(I32)


def _route(idx_all, tm, tm_e, nblk):
    n = idx_all.shape[0]
    nt = n // tm
    r = jnp.arange(tm)
    ltri = (r[:, None] > r[None, :]).astype(BF16)
    c = jnp.arange(LANES)
    utri = ((c[:, None] <= c[None, :]) & (c[:, None] < N_EXPERTS)).astype(F32)
    nblk_pad = -(-nblk // SUBLANES) * SUBLANES
    return pl.pallas_call(
        functools.partial(_route_kernel, tm_e),
        out_shape=(jax.ShapeDtypeStruct((n, LANES), I32),
                   jax.ShapeDtypeStruct((nblk_pad, LANES), I32),
                   jax.ShapeDtypeStruct((SUBLANES, LANES), I32)),
        grid=(2, nt),
        in_specs=[pl.BlockSpec((tm, LANES), lambda ph, i: (i, 0)),
                  pl.BlockSpec((tm, tm), lambda ph, i: (0, 0)),
                  pl.BlockSpec((LANES, LANES), lambda ph, i: (0, 0))],
        out_specs=(pl.BlockSpec((tm, LANES), lambda ph, i: (i * ph, 0)),
                   pl.BlockSpec((nblk_pad, LANES), lambda ph, i: (0, 0)),
                   pl.BlockSpec((SUBLANES, LANES), lambda ph, i: (0, 0))),
        scratch_shapes=[pltpu.VMEM((nt, tm, LANES), F32), pltpu.VMEM((SUBLANES, LANES), F32),
                        pltpu.VMEM((SUBLANES, LANES), F32)],
        compiler_params=_params(("arbitrary", "arbitrary")),
        name="route",
    )(idx_all, ltri, utri)


def _start_row_copies(tm, make_copy):
    def start(t, carry):
        for k in range(TOP_K):
            make_copy(t, k).start(priority=k % 2)
        return carry

    lax.fori_loop(0, tm, start, 0)


def _wait_row_copies(tm, make_copy):
    def wait(t, carry):
        for k in range(TOP_K):
            make_copy(t, k).wait()
        return carry

    lax.fori_loop(0, tm, wait, 0)


def _dispatch_kernel(dest_ref, x_ref, xs_in_hbm, xs_hbm, sem):
    del xs_in_hbm
    tm = x_ref.shape[0] // SUBLANES

    def make_copy(t, k):
        d = dest_ref[0, 0, t * TOP_K + k]
        return pltpu.make_async_copy(x_ref.at[_row_tile(t)], xs_hbm.at[_row_tile(d)], sem)

    _start_row_copies(tm, make_copy)
    _wait_row_copies(tm, make_copy)


def _dispatch(dest3, x_tiled, xs):
    nt, _, width = dest3.shape
    return pl.pallas_call(
        _dispatch_kernel,
        out_shape=jax.ShapeDtypeStruct(xs.shape, xs.dtype),
        grid=(nt,),
        in_specs=[pl.BlockSpec((1, 1, width), lambda i: (i, 0, 0), memory_space=pltpu.SMEM),
                  pl.BlockSpec((width // TOP_K * SUBLANES, LANES), lambda i: (i, 0)),
                  pl.BlockSpec(memory_space=pl.ANY)],
        out_specs=pl.BlockSpec(memory_space=pl.ANY),
        scratch_shapes=[pltpu.SemaphoreType.DMA(())],
        input_output_aliases={2: 0},
        compiler_params=_params(("arbitrary",)),
        name="dispatch",
    )(dest3, x_tiled, xs)


def _experts_kernel(be_ref, nused_ref, x_ref, wi_ref, bi_ref, wo_ref, bo_ref, o_ref,
                    wi_scr, wo_scr):
    i = pl.program_id(0)
    active = i < nused_ref[0]
    fresh = jnp.logical_or(i == 0, be_ref[i] != be_ref[jnp.maximum(i - 1, 0)])

    @pl.when(jnp.logical_and(active, fresh))
    def _():
        wi_scr[...] = wi_ref[0].astype(BF16)
        wo_scr[...] = wo_ref[0].astype(BF16)

    @pl.when(active)
    def _():
        rows = x_ref.shape[0] // SUBLANES
        x = _load_row_tiles(x_ref, rows)
        hdn = _dot(x.astype(BF16), wi_scr[...]) + bi_ref[0]
        gate = jnp.minimum(hdn[:, :D_FF], SWIGLU_LIMIT)
        up = jnp.clip(hdn[:, D_FF:], -SWIGLU_LIMIT, SWIGLU_LIMIT)
        act = (up + 1.0) * gate * _sigmoid(SWIGLU_ALPHA * gate)
        _store_row_tiles(o_ref, _dot(act.astype(BF16), wo_scr[...]) + bo_ref[0])

    @pl.when(jnp.logical_not(active))
    def _():
        o_ref[...] = jnp.zeros_like(o_ref)


def _experts(block_e, nused, xs, w_exp_in, b_exp_in, w_exp_out, b_exp_out, tm_e):
    n_rows = xs.shape[0] // SUBLANES
    tiled = pl.BlockSpec((tm_e * SUBLANES, LANES), lambda i, be, nu: (i, 0))
    grid_spec = pltpu.PrefetchScalarGridSpec(
        num_scalar_prefetch=2,
        grid=(n_rows // tm_e,),
        in_specs=[
            tiled,
            pl.BlockSpec((1, D_MODEL, 2 * D_FF), lambda i, be, nu: (be[i], 0, 0)),
            pl.BlockSpec((1, 1, 2 * D_FF), lambda i, be, nu: (be[i], 0, 0)),
            pl.BlockSpec((1, D_FF, D_MODEL), lambda i, be, nu: (be[i], 0, 0)),
            pl.BlockSpec((1, 1, D_MODEL), lambda i, be, nu: (be[i], 0, 0)),
        ],
        out_specs=tiled,
        scratch_shapes=[pltpu.VMEM((D_MODEL, 2 * D_FF), BF16), pltpu.VMEM((D_FF, D_MODEL), BF16)],
    )
    return pl.pallas_call(
        _experts_kernel,
        out_shape=jax.ShapeDtypeStruct(xs.shape, F32),
        grid_spec=grid_spec,
        compiler_params=_params(("arbitrary",)),
        name="experts",
    )(block_e, nused, xs, w_exp_in, b_exp_in, w_exp_out, b_exp_out)


def _final_kernel(dest_ref, dest_next_ref, h1_ref, gate_ref, p_ref, wpi_ref, wpg_ref, ys_hbm,
                  o_ref, rows_scr, sems):
    i = pl.program_id(0)
    tm = h1_ref.shape[0]
    slot = lax.rem(i, 2)

    def copies(idx_ref, half):
        def make_copy(t, k):
            d = idx_ref[0, 0, t * TOP_K + k]
            return pltpu.make_async_copy(ys_hbm.at[_row_tile(d)],
                                         rows_scr.at[half, k, _row_tile(t)], sems.at[half])
        return make_copy

    @pl.when(i == 0)
    def _():
        _start_row_copies(tm, copies(dest_ref, 0))

    @pl.when(i + 1 < pl.num_programs(0))
    def _():
        _start_row_copies(tm, copies(dest_next_ref, 1 - slot))

    _wait_row_copies(tm, copies(dest_ref, slot))
    gates = gate_ref[...]
    y = functools.reduce(jnp.add, [gates[:, k:k + 1] * _load_row_tiles(rows_scr, tm, (slot, k))
                                   for k in range(TOP_K)])
    h2 = h1_ref[...] + y
    ms = jnp.mean(h2 * h2, axis=-1, keepdims=True)
    hn = (h2 * lax.rsqrt(ms + EPS)).astype(BF16)
    gate = _sigmoid(_dot(hn, wpg_ref[...]))
    o_ref[...] = h2 + gate * _dot(p_ref[...].astype(BF16), wpi_ref[...])


def _final(dest3, h1, gates, p2d, ys, prm):
    nt, _, width = dest3.shape
    tm = width // TOP_K
    row = lambda i: (i, 0)
    const = lambda i: (0, 0)
    return pl.pallas_call(
        _final_kernel,
        out_shape=jax.ShapeDtypeStruct(h1.shape, F32),
        grid=(nt,),
        in_specs=[pl.BlockSpec((1, 1, width), lambda i: (i, 0, 0), memory_space=pltpu.SMEM),
                  pl.BlockSpec((1, 1, width), lambda i: (jnp.minimum(i + 1, nt - 1), 0, 0),
                               memory_space=pltpu.SMEM),
                  pl.BlockSpec((tm, D_MODEL), row),
                  pl.BlockSpec((tm, LANES), row),
                  pl.BlockSpec((tm, PLE_DIM), row),
                  pl.BlockSpec((PLE_DIM, D_MODEL), const),
                  pl.BlockSpec((D_MODEL, D_MODEL), const),
                  pl.BlockSpec(memory_space=pl.ANY)],
        out_specs=pl.BlockSpec((tm, D_MODEL), row),
        scratch_shapes=[pltpu.VMEM((2, TOP_K, tm * SUBLANES, LANES), F32),
                        pltpu.SemaphoreType.DMA((2,))],
        compiler_params=_params(("arbitrary",)),
        name="final",
    )(dest3, dest3, h1, gates, p2d, prm["w_ple_in"], prm["w_ple_gate"], ys)


TOKEN_TILE = 256
PROJ_ROW_TILE = 1024
GLA_BLOCK = 256
EXPERT_TILE = 256
ROUTE_TILE_MAX = 2048
SAMPLE_PAD = SUBLANES


def kernel(x_prompt, x_sample, cache_da1_kv, cache_da2_kv, cache_da3_kv, state_gla, p_prompt, p_sample, norm_mix_g, w_in, w_gla_alpha, b_gla_alpha, gla_norm_g, w_gla_out, qk_norm_q, qk_norm_k, w_da_out, w_o, norm_moe_g, w_router, b_router, w_exp_in, b_exp_in, w_exp_out, b_exp_out, w_ple_in, w_ple_gate):
    depth = w_in.shape[0]
    assert depth == 1, "single-layer trunk"
    l = 0
    prm = _prep_params(norm_mix_g[l], w_in[l], w_gla_alpha[l], b_gla_alpha[l], gla_norm_g[l],
                       w_gla_out[l], qk_norm_q[l], qk_norm_k[l], w_da_out[l], w_o[l], norm_moe_g[l],
                       w_router[l], b_router[l], b_exp_in[l], b_exp_out[l], w_ple_in[l], w_ple_gate[l])
    n_b, seq, _ = x_prompt.shape
    d_b, t_new, _ = x_sample.shape
    caches = (cache_da1_kv[l], cache_da2_kv[l], cache_da3_kv[l])
    assert seq % (DA_SPAN * DA_DILATIONS[-1]) == 0 and seq % PROJ_ROW_TILE == 0
    assert t_new <= SAMPLE_PAD and (d_b * SAMPLE_PAD) % TOKEN_TILE == 0
    for g in range(N_DA_GROUPS):
        assert caches[g].shape[1] == DA_WINDOWS[g]

    hp = x_prompt.reshape(n_b * seq, D_MODEL)
    pp, la_p = _proj(hp, prm, PROJ_ROW_TILE, dilate=True, p_dtype=BF16)
    zero_state = jnp.zeros((n_b, GLA_HEADS, GLA_DK, GLA_DV), F32)
    og_p, state_p = _gla(pp, la_p, zero_state, prm["gla_norm_g"], n_b, seq, GLA_BLOCK, GLA_CHUNK)
    da_p = [_da_prompt(pp, g, n_b, seq, PROJ_ROW_TILE) for g in range(N_DA_GROUPS)]
    h1_p, xn_p, idx_p, gate_p = _mix(pp, og_p, [o for o, _ in da_p], [s for _, s in da_p], hp, prm,
                                     TOKEN_TILE, PROJ_ROW_TILE)

    pad_t = ((0, 0), (0, SAMPLE_PAD - t_new), (0, 0))
    hs = jnp.pad(x_sample, pad_t).reshape(d_b * SAMPLE_PAD, D_MODEL)
    ps, la_s = _proj(hs, prm, TOKEN_TILE, dilate=False, p_dtype=F32)
    og_s, state_s = _gla(ps, la_s, state_gla[l], prm["gla_norm_g"], d_b, SAMPLE_PAD, SAMPLE_PAD,
                         SAMPLE_PAD, t_valid=t_new)
    caches_t = [jnp.transpose(c, (0, 2, 3, 4, 1)) for c in caches]
    o_s, *new_caches_t = _da_sample(ps, caches_t, d_b, SAMPLE_PAD, t_new)
    h1_s, xn_s, idx_s, gate_s = _mix(ps, og_s, [o_s], [], hs, prm, TOKEN_TILE)

    n_p, n_s = hp.shape[0], hs.shape[0]
    n_rows = (n_p + n_s) * TOP_K + N_EXPERTS * EXPERT_TILE
    route_tile = max(c for c in range(TOKEN_TILE, ROUTE_TILE_MAX + 1, TOKEN_TILE)
                     if (n_p + n_s) % c == 0)
    dest, block_e, nused = _route(jnp.concatenate([idx_p, idx_s], axis=0), route_tile, EXPERT_TILE,
                                  n_rows // EXPERT_TILE)
    dest3 = dest[:, :TOP_K].reshape(-1, 1, TOKEN_TILE * TOP_K)
    dest3_p, dest3_s = dest3[:n_p // TOKEN_TILE], dest3[n_p // TOKEN_TILE:]
    xs = jnp.zeros((n_rows * SUBLANES, LANES), F32)
    xs = _dispatch(dest3_p, xn_p, xs)
    xs = _dispatch(dest3_s, xn_s, xs)
    ys = _experts(block_e[:n_rows // EXPERT_TILE, 0], nused[0, :1], xs, w_exp_in[l], prm["b_exp_in"],
                  w_exp_out[l], prm["b_exp_out"], EXPERT_TILE)
    out_p = _final(dest3_p, h1_p, gate_p, p_prompt[l].reshape(n_p, PLE_DIM), ys, prm)
    out_s = _final(dest3_s, h1_s, gate_s, jnp.pad(p_sample[l], pad_t).reshape(n_s, PLE_DIM), ys, prm)

    y_prompt = out_p.reshape(n_b, seq, D_MODEL)
    y_sample = out_s.reshape(d_b, SAMPLE_PAD, D_MODEL)[:, :t_new]
    outs_p, outs_s = [], []
    for g in range(N_DA_GROUPS):
        win, dil = min(DA_WINDOWS[g], seq), DA_DILATIONS[g]
        ck = (COL_DA + 3 * g + 1) * PROJ_TN
        n_tiles = -(-win // PROJ_ROW_TILE)
        kv_p = pp.reshape(n_b, seq // PROJ_ROW_TILE, dil, PROJ_ROW_TILE // dil, PROJ_WIDTH)
        kv_p = kv_p[:, -n_tiles:, :, :, ck:ck + 2 * DA_WIDTH]
        kv_p = jnp.transpose(kv_p, (0, 1, 3, 2, 4)).reshape(n_b, n_tiles * PROJ_ROW_TILE, -1)
        outs_p.append(kv_p[:, -win:].astype(F32).reshape(1, n_b, win, 2, DA_HEADS, DA_HEAD_DIM))
        outs_s.append(jnp.transpose(new_caches_t[g], (0, 4, 1, 2, 3))[None])
    return (y_prompt, y_sample, state_p[None], state_s[None], outs_p[0], outs_s[0], outs_p[1],
            outs_s[1], outs_p[2], outs_s[2])
```

```python
import functools

import jax
import jax.numpy as jnp
from jax import lax
from jax.experimental import pallas as pl
from jax.experimental.pallas import tpu as pltpu

F32 = jnp.float32
BF16 = jnp.bfloat16
I32 = jnp.int32

D_MODEL = 1024
GLA_HEADS = 4
GLA_DK = 128
GLA_DV = 256
GLA_QK = GLA_HEADS * GLA_DK
GLA_V = GLA_HEADS * GLA_DV
GLA_RANK = 16
GLA_TAU = 16.0
GLA_CHUNK = 32
DA_WINDOWS = (128, 512, 2048)
DA_DILATIONS = (1, 4, 16)
DA_SPAN = 128
N_DA_GROUPS = 3
DA_HEADS = 8
DA_HEAD_DIM = 64
DA_WIDTH = DA_HEADS * DA_HEAD_DIM
N_EXPERTS = 32
TOP_K = 4
D_FF = D_MODEL
SWIGLU_LIMIT = 7.0
SWIGLU_ALPHA = 1.702
PLE_DIM = 256
EPS = 1e-6

LANES = 128
SUBLANES = 8
VMEM_LIMIT = 56 * 1024 * 1024

PROJ_TN = 512
COL_GQ, COL_GK, COL_GV, COL_GR, COL_GA, COL_GB, COL_DA = 0, 1, 2, 4, 6, 8, 10
PROJ_TILES = COL_DA + 3 * N_DA_GROUPS
PROJ_WIDTH = PROJ_TILES * PROJ_TN
NEG_BIG = -1e30

HIGHEST = lax.Precision.HIGHEST


def _dot(a, b):
    return jnp.dot(a, b, preferred_element_type=F32)


def _dot_nt(a, b):
    return lax.dot_general(a, b, (((1,), (1,)), ((), ())), preferred_element_type=F32)


def _dot_tn(a, b):
    return lax.dot_general(a, b, (((0,), (0,)), ((), ())), preferred_element_type=F32)


def _sigmoid(x):
    return 1.0 / (1.0 + jnp.exp(-x))


def _params(sem, vmem=VMEM_LIMIT):
    return pltpu.CompilerParams(dimension_semantics=sem, vmem_limit_bytes=vmem)


assert D_MODEL == SUBLANES * LANES


def _row_tile(r):
    return pl.ds(pl.multiple_of(r * SUBLANES, SUBLANES), SUBLANES)


def _store_row_tiles(ref, value, lead=()):
    rows = value.shape[0]
    for s in range(SUBLANES):
        ref[(*lead, pl.ds(s, rows, stride=SUBLANES), slice(None))] = value[:, s * LANES:(s + 1) * LANES]


def _load_row_tiles(ref, rows, lead=()):
    return jnp.concatenate(
        [ref[(*lead, pl.ds(s, rows, stride=SUBLANES), slice(None))] for s in range(SUBLANES)], axis=1)


def _proj_kernel(dilate, x_ref, g_ref, w_ref, wlr_ref, wal_ref, bal_ref, qkg_ref, bd_ref,
                 p_ref, la_ref, xn_scr, stage_scr):
    j = pl.program_id(1)
    tm = x_ref.shape[0]

    @pl.when(j == 0)
    def _():
        x = x_ref[...]
        ms = jnp.mean(x * x, axis=-1, keepdims=True)
        xn = (x * lax.rsqrt(ms + EPS) * g_ref[...]).astype(BF16)
        xn_scr[...] = xn
        glr = _dot(xn, wlr_ref[...])
        z = _dot(glr.astype(BF16), wal_ref[...]) + bal_ref[...]
        log_sig = jnp.minimum(z, 0.0) - jnp.log(1.0 + jnp.exp(-jnp.abs(z)))
        la_ref[...] = log_sig * (1.0 / GLA_TAU)

    acc = _dot(xn_scr[...], w_ref[...])
    is_qk = jnp.logical_and(j >= COL_DA, (j - COL_DA) % 3 != 2)
    group = (j - COL_DA) // 3
    staged = [jnp.logical_and(j >= COL_DA, group == g) if dilate and DA_DILATIONS[g] > 1 else None
              for g in range(N_DA_GROUPS)]
    any_staged = functools.reduce(jnp.logical_or, [s for s in staged if s is not None], False)

    def qk_normed():
        ssum = _dot((acc * acc).astype(BF16), bd_ref[...])
        return acc * lax.rsqrt(ssum * (1.0 / DA_HEAD_DIM) + EPS) * qkg_ref[...]

    def emit(value):
        if any_staged is False:
            p_ref[...] = value.astype(p_ref.dtype)
            return

        @pl.when(any_staged)
        def _():
            for c in range(PROJ_TN // LANES):
                stage_scr[c] = value[:, c * LANES:(c + 1) * LANES]

        @pl.when(jnp.logical_not(any_staged))
        def _():
            p_ref[...] = value.astype(p_ref.dtype)

    @pl.when(is_qk)
    def _():
        emit(qk_normed())

    @pl.when(jnp.logical_not(is_qk))
    def _():
        emit(acc)

    for g in range(N_DA_GROUPS):
        if staged[g] is None:
            continue
        dil = DA_DILATIONS[g]
        per = tm // dil

        @pl.when(staged[g])
        def _(dil=dil, per=per):
            for r in range(dil):
                for c in range(PROJ_TN // LANES):
                    p_ref[r * per:(r + 1) * per, c * LANES:(c + 1) * LANES] = (
                        stage_scr[c, pl.ds(r, per, stride=dil), :].astype(p_ref.dtype))


def _proj(x2d, prm, tm, dilate, p_dtype):
    m = x2d.shape[0]
    grid = (m // tm, PROJ_TILES)
    const = lambda i, j: (0, 0)
    return pl.pallas_call(
        functools.partial(_proj_kernel, dilate),
        out_shape=(jax.ShapeDtypeStruct((m, PROJ_WIDTH), p_dtype),
                   jax.ShapeDtypeStruct((m, GLA_QK), F32)),
        grid=grid,
        in_specs=[
            pl.BlockSpec((tm, D_MODEL), lambda i, j: (i, 0)),
            pl.BlockSpec((1, D_MODEL), const),
            pl.BlockSpec((D_MODEL, PROJ_TN), lambda i, j: (0, j)),
            pl.BlockSpec((D_MODEL, LANES), const),
            pl.BlockSpec((LANES, GLA_QK), const),
            pl.BlockSpec((1, GLA_QK), const),
            pl.BlockSpec((1, PROJ_TN), lambda i, j: (0, jnp.maximum(j - COL_DA, 0))),
            pl.BlockSpec((PROJ_TN, PROJ_TN), const),
        ],
        out_specs=(pl.BlockSpec((tm, PROJ_TN), lambda i, j: (i, j)),
                   pl.BlockSpec((tm, GLA_QK), lambda i, j: (i, 0))),
        scratch_shapes=[pltpu.VMEM((tm, D_MODEL), BF16),
                        pltpu.VMEM((PROJ_TN // LANES, tm, LANES), F32)],
        compiler_params=_params(("arbitrary", "arbitrary")),
        name="proj",
    )(x2d, prm["norm_mix_g"], prm["w_main"], prm["w_lr"], prm["w_alpha"], prm["b_alpha"],
      prm["qk_gain"], prm["head_ones"])


def _gla_kernel(chunk, t_valid, q_ref, k_ref, v_ref, r_ref, la_ref, s0_ref, g_ref, tri_ref,
                o_ref, s_ref, s_scr):
    t = pl.program_id(1)
    tb = q_ref.shape[0]
    n_chunks = tb // chunk
    row_ok = lax.broadcasted_iota(I32, (tb, GLA_DK), 0) < t_valid

    @pl.when(t == 0)
    def _():
        for h in range(GLA_HEADS):
            s_scr[h] = s0_ref[0, h].T

    tri = tri_ref[...]
    causal = tri > 0
    for h in range(GLA_HEADS):
        ks = slice(h * GLA_DK, (h + 1) * GLA_DK)
        vs = slice(h * GLA_DV, (h + 1) * GLA_DV)
        la = la_ref[:, ks]
        if t_valid < tb:
            la = jnp.where(row_ok, la, 0.0)
        la_hi = la.astype(BF16)
        la_lo = (la - la_hi.astype(F32)).astype(BF16)
        cum = _dot(tri, la_hi) + _dot(tri, la_lo)
        q = q_ref[:, ks].astype(F32) * (GLA_DK ** -0.5)
        k = k_ref[:, ks].astype(F32)
        v = v_ref[:, vs].astype(BF16)
        q_dec = (q * jnp.exp(cum)).astype(BF16)
        k_inv = (k * jnp.exp(-cum)).astype(BF16)
        att = jnp.where(causal, _dot_nt(q_dec, k_inv), 0.0)
        o_intra = _dot(att.astype(BF16), v)
        s_t = s_scr[h]
        for c in range(n_chunks):
            rows = slice(c * chunk, (c + 1) * chunk)
            tot = cum[(c + 1) * chunk - 1:(c + 1) * chunk, :]
            k_end = (k[rows] * jnp.exp(tot - cum[rows])).astype(BF16)
            o_c = o_intra[rows] + _dot_nt(q_dec[rows], s_t.astype(BF16))
            s_t = s_t * jnp.exp(tot) + _dot_tn(v[rows], k_end)
            ms = jnp.mean(o_c * o_c, axis=-1, keepdims=True)
            gate = r_ref[rows, vs].astype(F32)
            o_ref[rows, vs] = o_c * lax.rsqrt(ms + EPS) * g_ref[...] * (gate * _sigmoid(gate))
        s_scr[h] = s_t

    @pl.when(t == pl.num_programs(1) - 1)
    def _():
        for h in range(GLA_HEADS):
            s_ref[0, h] = s_scr[h].T


def _gla(p, la, s0, gla_norm_g, n_batch, t_len, tb, chunk, t_valid=None):
    nt = t_len // tb
    t_valid = tb if t_valid is None else t_valid
    r = jnp.arange(tb)
    same = (r[:, None] // chunk) == (r[None, :] // chunk)
    tri = (same & (r[:, None] >= r[None, :])).astype(BF16)
    row = lambda b, t: b * nt + t
    const = lambda b, t: (0, 0)
    return pl.pallas_call(
        functools.partial(_gla_kernel, chunk, t_valid),
        out_shape=(jax.ShapeDtypeStruct((n_batch * t_len, GLA_V), F32),
                   jax.ShapeDtypeStruct((n_batch, GLA_HEADS, GLA_DK, GLA_DV), F32)),
        grid=(n_batch, nt),
        in_specs=[
            pl.BlockSpec((tb, GLA_QK), lambda b, t: (row(b, t), COL_GQ)),
            pl.BlockSpec((tb, GLA_QK), lambda b, t: (row(b, t), COL_GK)),
            pl.BlockSpec((tb, GLA_V), lambda b, t: (row(b, t), COL_GV * PROJ_TN // GLA_V)),
            pl.BlockSpec((tb, GLA_V), lambda b, t: (row(b, t), COL_GR * PROJ_TN // GLA_V)),
            pl.BlockSpec((tb, GLA_QK), lambda b, t: (row(b, t), 0)),
            pl.BlockSpec((1, GLA_HEADS, GLA_DK, GLA_DV), lambda b, t: (b, 0, 0, 0)),
            pl.BlockSpec((1, GLA_DV), const),
            pl.BlockSpec((tb, tb), const),
        ],
        out_specs=(pl.BlockSpec((tb, GLA_V), lambda b, t: (row(b, t), 0)),
                   pl.BlockSpec((1, GLA_HEADS, GLA_DK, GLA_DV), lambda b, t: (b, 0, 0, 0))),
        scratch_shapes=[pltpu.VMEM((GLA_HEADS, GLA_DV, GLA_DK), F32)],
        compiler_params=_params(("arbitrary", "arbitrary")),
        name="gla",
    )(p, p, p, p, la, s0, gla_norm_g, tri)


def _prep_params(norm_mix_g, w_in, w_gla_alpha, b_gla_alpha, gla_norm_g, w_gla_out, qk_norm_q,
                 qk_norm_k, w_da_out, w_o, norm_moe_g, w_router, b_router, b_exp_in, b_exp_out,
                 w_ple_in, w_ple_gate):
    o_lr = GLA_QK * 2 + GLA_V * 2
    o_da = o_lr + GLA_RANK
    o_ga = o_da + 3 * N_DA_GROUPS * DA_WIDTH
    w_main = jnp.concatenate([w_in[:, :o_lr], w_in[:, o_ga:], w_in[:, o_da:o_ga]], axis=1).astype(BF16)
    w_lr = jnp.pad(w_in[:, o_lr:o_da], ((0, 0), (0, LANES - GLA_RANK))).astype(BF16)
    w_alpha = jnp.pad(w_gla_alpha, ((0, LANES - GLA_RANK), (0, 0))).astype(BF16)
    gains = []
    for g in range(N_DA_GROUPS):
        gains.append(jnp.tile(qk_norm_q[g], DA_HEADS) * (DA_HEAD_DIM ** -0.5))
        gains.append(jnp.tile(qk_norm_k[g], DA_HEADS))
        gains.append(jnp.ones((DA_WIDTH,), F32))
    qk_gain = jnp.concatenate(gains)[None, :]
    c = jnp.arange(PROJ_TN)
    head_ones = (c[:, None] // DA_HEAD_DIM == c[None, :] // DA_HEAD_DIM).astype(BF16)
    pad_e = LANES - N_EXPERTS
    w_r = jnp.pad(w_router, ((0, 0), (0, pad_e)))
    w_r_hi = w_r.astype(BF16)
    w_r_split = jnp.stack([w_r_hi, (w_r - w_r_hi.astype(F32)).astype(BF16)])
    return dict(
        norm_mix_g=norm_mix_g[None, :], w_main=w_main, w_lr=w_lr, w_alpha=w_alpha,
        b_alpha=b_gla_alpha[None, :], qk_gain=qk_gain, head_ones=head_ones,
        gla_norm_g=gla_norm_g[None, :], w_gla_out=w_gla_out.astype(BF16),
        w_da_out=w_da_out.astype(BF16), w_o=w_o.astype(BF16), norm_moe_g=norm_moe_g[None, :],
        w_router=w_r_split,
        b_router=jnp.pad(b_router, (0, pad_e), constant_values=NEG_BIG)[None, :],
        b_exp_in=b_exp_in[:, None, :], b_exp_out=b_exp_out[:, None, :],
        w_ple_in=w_ple_in.astype(BF16), w_ple_gate=w_ple_gate.astype(BF16),
    )


def _da_prompt_kernel(q_ref, kc_ref, kp_ref, vc_ref, vp_ref, bias_ref, o_ref, lse_ref):
    band = bias_ref[1]
    first = jnp.where(pl.program_id(2) == 0, bias_ref[0], band)
    low = lax.broadcasted_iota(I32, (DA_SPAN, LANES), 1) < DA_HEAD_DIM
    pieces = q_ref.shape[1:4:2]
    n_q = pieces[0] * pieces[1] // DA_SPAN

    def rows(ref, cs):
        return ref[0, :, 0, :, cs].reshape(-1, LANES)

    for a in range(DA_WIDTH // LANES):
        cs = slice(a * LANES, (a + 1) * LANES)
        q_all = rows(q_ref, cs)
        k_all = jnp.concatenate([rows(kp_ref, cs), rows(kc_ref, cs)], axis=0).astype(BF16)
        v_all = jnp.concatenate([rows(vp_ref, cs), rows(vc_ref, cs)], axis=0).astype(BF16)
        outs, lses = [], []
        for j in range(n_q):
            q2 = q_all[j * DA_SPAN:(j + 1) * DA_SPAN]
            k2 = k_all[j * DA_SPAN:(j + 2) * DA_SPAN]
            v2 = v_all[j * DA_SPAN:(j + 2) * DA_SPAN]
            res = []
            for sel in (low, jnp.logical_not(low)):
                s = _dot_nt(jnp.where(sel, q2, 0.0).astype(BF16), k2) + (first if j == 0 else band)
                m = jnp.max(s, axis=-1, keepdims=True)
                p = jnp.exp(s - m)
                l = jnp.sum(p, axis=-1, keepdims=True)
                res.append((_dot(p.astype(BF16), v2) / l, m + jnp.log(l)))
            outs.append(jnp.where(low, res[0][0], res[1][0]))
            lses.append(jnp.where(low, res[0][1], res[1][1]))
        o_ref[0, :, 0, :, cs] = jnp.concatenate(outs, axis=0).reshape(*pieces, LANES)
        lse_ref[0, :, 0, :, cs] = jnp.concatenate(lses, axis=0).reshape(*pieces, LANES)


def _da_bias():
    qi = jnp.arange(DA_SPAN)[:, None]
    kj = jnp.arange(2 * DA_SPAN)[None, :]
    d = kj - qi
    band = (d >= 0) & (d <= DA_SPAN)
    first = band & (kj >= DA_SPAN)
    return jnp.where(jnp.stack([first, band]), 0.0, NEG_BIG).astype(F32)


def _da_prompt(p, g, n_batch, seq, tile):
    dil = DA_DILATIONS[g]
    per = tile // dil
    sub_len = seq // dil
    q_blocks = min(DA_QUERY_BLOCKS, sub_len // DA_SPAN)
    p5 = p.reshape(n_batch, seq // tile, dil, per, PROJ_WIDTH)
    cq, ck, cv = (COL_DA + 3 * g + i for i in range(3))

    def spec(c, rows, block_of_step):
        if per >= rows:
            assert per % rows == 0
            blk = (1, 1, 1, rows, DA_WIDTH)
            place = lambda m: (m // (per // rows), m % (per // rows))
        else:
            assert rows % per == 0 and per % SUBLANES == 0
            blk = (1, rows // per, 1, per, DA_WIDTH)
            place = lambda m: (m, 0)

        def index(b, r, n):
            tile_i, row_i = place(block_of_step(n))
            return (b, tile_i, r, row_i, c)
        return pl.BlockSpec(blk, index)

    cur = lambda c: spec(c, q_blocks * DA_SPAN, lambda n: n)
    prev = lambda c: spec(c, DA_SPAN, lambda n: jnp.maximum(q_blocks * n - 1, 0))
    out_sds = jax.ShapeDtypeStruct((n_batch, seq // tile, dil, per, DA_WIDTH), F32)
    return pl.pallas_call(
        _da_prompt_kernel,
        out_shape=(out_sds, out_sds),
        grid=(n_batch, dil, sub_len // (q_blocks * DA_SPAN)),
        in_specs=[cur(cq), cur(ck), prev(ck), cur(cv), prev(cv),
                  pl.BlockSpec((2, DA_SPAN, 2 * DA_SPAN), lambda b, r, n: (0, 0, 0))],
        out_specs=(cur(0), cur(0)),
        compiler_params=_params(("arbitrary", "arbitrary", "arbitrary")),
        name=f"da_prompt{g}",
    )(p5, p5, p5, p5, p5, _da_bias())


def _mix_kernel(dils, *refs):
    og_ref = refs[0]
    n_groups = len(dils) if dils else 1
    n_lse = n_groups if dils else 0
    o_refs = refs[1:1 + n_groups]
    l_refs = refs[1 + n_groups:1 + n_groups + n_lse]
    (ga_ref, gb_ref, h_ref, wga_ref, wda_ref, wo_ref, gm_ref, wr_ref, br_ref,
     h1_ref, xn_ref, idx_ref, gate_ref) = refs[1 + n_groups + n_lse:14 + n_groups + n_lse]
    scratch = iter(refs[14 + n_groups + n_lse:])

    def token_major(ref, dil):
        if dil == 1:
            return ref[0, 0]
        scr = next(scratch)
        per = ref.shape[2]
        for r in range(dil):
            for c in range(DA_WIDTH // LANES):
                scr[c, pl.ds(r, per, stride=dil), :] = ref[0, r, :, c * LANES:(c + 1) * LANES]
        return jnp.concatenate([scr[c] for c in range(DA_WIDTH // LANES)], axis=1)

    if not dils:
        o_da = o_refs[0][...]
    else:
        outs = [token_major(r, d) for r, d in zip(o_refs, dils)]
        lses = [token_major(r, d) for r, d in zip(l_refs, dils)]
        m = functools.reduce(jnp.maximum, lses)
        es = [jnp.exp(l - m) for l in lses]
        num = functools.reduce(jnp.add, [e * o for e, o in zip(es, outs)])
        o_da = num / functools.reduce(jnp.add, es)
    y_a = _dot(og_ref[...].astype(BF16), wga_ref[...])
    y_b = _dot(o_da.astype(BF16), wda_ref[...])
    y = _sigmoid(ga_ref[...].astype(F32)) * y_a + _sigmoid(gb_ref[...].astype(F32)) * y_b
    h1 = h_ref[...] + _dot(y.astype(BF16), wo_ref[...])
    h1_ref[...] = h1
    ms = jnp.mean(h1 * h1, axis=-1, keepdims=True)
    xn = h1 * lax.rsqrt(ms + EPS) * gm_ref[...]
    _store_row_tiles(xn_ref, xn)
    xn_hi = xn.astype(BF16)
    xn_lo = (xn - xn_hi.astype(F32)).astype(BF16)
    logits = (_dot(xn_hi, wr_ref[0]) + _dot(xn_lo, wr_ref[0]) + _dot(xn_hi, wr_ref[1])) + br_ref[...]
    lane = lax.broadcasted_iota(I32, logits.shape, 1)
    lane_f = lane.astype(F32)
    work = logits
    idx_out = jnp.zeros(logits.shape, I32)
    val_out = jnp.zeros(logits.shape, F32)
    vals = []
    for k in range(TOP_K):
        top = jnp.max(work, axis=-1, keepdims=True)
        pick = jnp.min(jnp.where(work == top, lane_f, float(LANES)), axis=-1, keepdims=True)
        pick = pick.astype(I32)
        vals.append(top)
        idx_out = jnp.where(lane == k, pick, idx_out)
        work = jnp.where(lane == pick, -3e38, work)
    es = [jnp.exp(v - vals[0]) for v in vals]
    den = functools.reduce(jnp.add, es)
    for k in range(TOP_K):
        val_out = jnp.where(lane == k, es[k] / den, val_out)
    idx_ref[...] = idx_out
    gate_ref[...] = val_out


def _mix(p, o_gla, o_das, lses, h2d, prm, tm, tile=None):
    m = h2d.shape[0]
    row = lambda i: (i, 0)
    const = lambda i: (0, 0)
    wide = pl.BlockSpec((tm, D_MODEL), row)
    small = pl.BlockSpec((tm, LANES), row)
    if lses:
        dils = DA_DILATIONS
        sub = tile // tm
        da_in, da_specs, scratch = [], [], []
        for arr, dil in list(zip(o_das, dils)) + list(zip(lses, dils)):
            assert (tm // dil) % SUBLANES == 0
            da_in.append(arr.reshape(-1, dil, tile // dil, DA_WIDTH))
            da_specs.append(pl.BlockSpec((1, dil, tm // dil, DA_WIDTH),
                                         lambda i: (i // sub, 0, i % sub, 0)))
            if dil > 1:
                scratch.append(pltpu.VMEM((DA_WIDTH // LANES, tm, LANES), F32))
    else:
        dils, da_in, scratch = None, list(o_das), []
        da_specs = [pl.BlockSpec((tm, DA_WIDTH), row)]
    return pl.pallas_call(
        functools.partial(_mix_kernel, dils),
        out_shape=(jax.ShapeDtypeStruct((m, D_MODEL), F32),
                   jax.ShapeDtypeStruct((m * SUBLANES, LANES), F32),
                   jax.ShapeDtypeStruct((m, LANES), I32), jax.ShapeDtypeStruct((m, LANES), F32)),
        grid=(m // tm,),
        scratch_shapes=scratch,
        in_specs=[wide] + da_specs + [
            pl.BlockSpec((tm, D_MODEL), lambda i: (i, COL_GA * PROJ_TN // D_MODEL)),
            pl.BlockSpec((tm, D_MODEL), lambda i: (i, COL_GB * PROJ_TN // D_MODEL)),
            wide,
            pl.BlockSpec((GLA_V, D_MODEL), const),
            pl.BlockSpec((DA_WIDTH, D_MODEL), const),
            pl.BlockSpec((D_MODEL, D_MODEL), const),
            pl.BlockSpec((1, D_MODEL), const),
            pl.BlockSpec((2, D_MODEL, LANES), lambda i: (0, 0, 0)),
            pl.BlockSpec((1, LANES), const),
        ],
        out_specs=(wide, pl.BlockSpec((tm * SUBLANES, LANES), row), small, small),
        compiler_params=_params(("arbitrary",)),
        name="mix",
    )(o_gla, *da_in, p, p, h2d, prm["w_gla_out"], prm["w_da_out"], prm["w_o"],
      prm["norm_moe_g"], prm["w_router"], prm["b_router"])


def _da_sample_kernel(t_new, *refs):
    qkv = [refs[3 * g:3 * g + 3] for g in range(N_DA_GROUPS)]
    c_refs = refs[9:12]
    sel_ref = refs[12]
    o_ref = refs[13]
    new_refs = refs[14:17]
    t_pad = o_ref.shape[0]
    q_row = lax.broadcasted_iota(I32, (t_pad, t_pad), 0)
    k_row = lax.broadcasted_iota(I32, (t_pad, t_pad), 1)
    lane = lax.broadcasted_iota(I32, (DA_HEAD_DIM, LANES), 1)
    heads = []
    for hh in range(LANES // DA_HEAD_DIM):
        hs = slice(hh * DA_HEAD_DIM, (hh + 1) * DA_HEAD_DIM)
        lses, accs, dens = [], [], []
        for g in range(N_DA_GROUPS):
            dil, win = DA_DILATIONS[g], DA_WINDOWS[g]
            q_ref, k_ref, v_ref = qkv[g]
            q = q_ref[:, hs].astype(BF16)
            k_new = k_ref[:, hs]
            v_new = v_ref[:, hs]
            kt = c_refs[g][0, 0, hh]
            vt = c_refs[g][0, 1, hh]
            t_idx = lax.broadcasted_iota(I32, (t_pad, win), 0)
            off = lax.broadcasted_iota(I32, (t_pad, win), 1) - t_idx
            ok = jnp.logical_and(off >= 0, jnp.bitwise_and(off, dil - 1) == 0)
            s = jnp.where(ok, _dot(q, kt.astype(BF16)), NEG_BIG)
            back = q_row - k_row
            ok_new = jnp.logical_and(jnp.logical_and(back >= 0, jnp.bitwise_and(back, dil - 1) == 0),
                                     k_row < t_new)
            s_new = jnp.where(ok_new, _dot_nt(q, k_new.astype(BF16)), NEG_BIG)
            m = jnp.maximum(jnp.max(s, axis=-1, keepdims=True), jnp.max(s_new, axis=-1, keepdims=True))
            p = jnp.exp(s - m)
            p_new = jnp.exp(s_new - m)
            dens.append(jnp.sum(p, axis=-1, keepdims=True) + jnp.sum(p_new, axis=-1, keepdims=True))
            accs.append(_dot_nt(p.astype(BF16), vt.astype(BF16))
                        + _dot(p_new.astype(BF16), v_new.astype(BF16)))
            lses.append(m + jnp.log(dens[-1]))
            for kv, (old, new) in enumerate(((kt, k_new), (vt, v_new))):
                moved = pltpu.roll(old, win - t_new, axis=1)
                tail = lax.dot_general(new, sel_ref[...], (((0,), (0,)), ((), ())),
                                       preferred_element_type=F32, precision=HIGHEST)
                tail = jnp.where(lane >= LANES - t_new, tail, moved[:, win - LANES:])
                if win > LANES:
                    new_refs[g][0, kv, hh, :, :win - LANES] = moved[:, :win - LANES]
                new_refs[g][0, kv, hh, :, win - LANES:] = tail
        top = functools.reduce(jnp.maximum, lses)
        ws = [jnp.exp(l - top) for l in lses]
        wsum = functools.reduce(jnp.add, ws)
        heads.append(functools.reduce(
            jnp.add, [accs[g] * (ws[g] / (wsum * dens[g])) for g in range(N_DA_GROUPS)]))
    o_ref[...] = jnp.concatenate(heads, axis=1)


def _da_sample(p_s, caches_t, n_batch, t_pad, t_new):
    pair = LANES // DA_HEAD_DIM
    t = jnp.arange(t_pad)[:, None]
    sel = ((jnp.arange(LANES)[None, :] == LANES - t_new + t) & (t < t_new)).astype(F32)
    per_tile = PROJ_TN // LANES
    qkv_specs = [pl.BlockSpec((t_pad, LANES), lambda b, a, col=COL_DA + i: (b, col * per_tile + a))
                 for i in range(3 * N_DA_GROUPS)]
    c_specs = [pl.BlockSpec((1, 2, pair, DA_HEAD_DIM, DA_WINDOWS[g]), lambda b, a: (b, 0, a, 0, 0))
               for g in range(N_DA_GROUPS)]
    return pl.pallas_call(
        functools.partial(_da_sample_kernel, t_new),
        out_shape=(jax.ShapeDtypeStruct((n_batch * t_pad, DA_WIDTH), F32),
                   *[jax.ShapeDtypeStruct(c.shape, F32) for c in caches_t]),
        grid=(n_batch, DA_HEADS // pair),
        in_specs=qkv_specs + c_specs + [pl.BlockSpec((t_pad, LANES), lambda b, a: (0, 0))],
        out_specs=(pl.BlockSpec((t_pad, LANES), lambda b, a: (b, a)), *c_specs),
        compiler_params=_params(("arbitrary", "arbitrary")),
        name="da_sample",
    )(*([p_s] * (3 * N_DA_GROUPS)), *caches_t, sel)


def _route_kernel(tm_e, idx_ref, ltri_ref, utri_ref, dest_ref, be_ref, nused_ref,
                  rank_scr, carry_scr, pstart_scr):
    ph = pl.program_id(0)
    i = pl.program_id(1)
    tm = idx_ref.shape[0]
    lane = lax.broadcasted_iota(I32, (tm, LANES), 1)
    idx = idx_ref[...]
    hots = [lane == idx[:, k:k + 1] for k in range(TOP_K)]

    @pl.when(ph == 0)
    def _():
        @pl.when(i == 0)
        def _():
            carry_scr[...] = jnp.zeros_like(carry_scr)

        cnt = functools.reduce(jnp.add, [h.astype(F32) for h in hots])
        before = _dot(ltri_ref[...], cnt.astype(BF16)) + carry_scr[0:1, :]
        rank = jnp.zeros((tm, LANES), F32)
        for k in range(TOP_K):
            r_k = jnp.sum(jnp.where(hots[k], before, 0.0), axis=-1, keepdims=True)
            rank = jnp.where(lane == k, r_k, rank)
        rank_scr[i] = rank
        carry_scr[...] = carry_scr[...] + jnp.sum(cnt, axis=0, keepdims=True)

    @pl.when(ph == 1)
    def _():
        @pl.when(i == 0)
        def _():
            counts = carry_scr[...]
            padded = jnp.floor((counts + (tm_e - 1)) * (1.0 / tm_e)) * tm_e
            pend = jnp.dot(padded, utri_ref[...], preferred_element_type=F32, precision=HIGHEST)
            pstart_scr[...] = pend - padded
            nblk = be_ref.shape[0]
            start = lax.broadcasted_iota(I32, (nblk, LANES), 0).astype(F32) * tm_e
            is_e = lax.broadcasted_iota(I32, (nblk, LANES), 1) < N_EXPERTS
            done = jnp.where(jnp.logical_and(pend[0:1, :] <= start, is_e), 1.0, 0.0)
            be = jnp.minimum(jnp.sum(done, axis=-1, keepdims=True), N_EXPERTS - 1.0)
            be_ref[...] = jnp.broadcast_to(be, (nblk, LANES)).astype(I32)
            total = pend[:, N_EXPERTS - 1:N_EXPERTS] * (1.0 / tm_e)
            nused_ref[...] = jnp.broadcast_to(total, nused_ref.shape).astype(I32)

        dest = rank_scr[i]
        for k in range(TOP_K):
            d_k = jnp.sum(jnp.where(hots[k], pstart_scr[0:1, :], 0.0), axis=-1, keepdims=True)
            dest = dest + jnp.where(lane == k, d_k, 0.0)
        dest_ref[...] = dest.astype(I32)


def _route(idx_all, tm, tm_e, nblk):
    n = idx_all.shape[0]
    nt = n // tm
    r = jnp.arange(tm)
    ltri = (r[:, None] > r[None, :]).astype(BF16)
    c = jnp.arange(LANES)
    utri = ((c[:, None] <= c[None, :]) & (c[:, None] < N_EXPERTS)).astype(F32)
    nblk_pad = -(-nblk // SUBLANES) * SUBLANES
    return pl.pallas_call(
        functools.partial(_route_kernel, tm_e),
        out_shape=(jax.ShapeDtypeStruct((n, LANES), I32),
                   jax.ShapeDtypeStruct((nblk_pad, LANES), I32),
                   jax.ShapeDtypeStruct((SUBLANES, LANES), I32)),
        grid=(2, nt),
        in_specs=[pl.BlockSpec((tm, LANES), lambda ph, i: (i, 0)),
                  pl.BlockSpec((tm, tm), lambda ph, i: (0, 0)),
                  pl.BlockSpec((LANES, LANES), lambda ph, i: (0, 0))],
        out_specs=(pl.BlockSpec((tm, LANES), lambda ph, i: (i * ph, 0)),
                   pl.BlockSpec((nblk_pad, LANES), lambda ph, i: (0, 0)),
                   pl.BlockSpec((SUBLANES, LANES), lambda ph, i: (0, 0))),
        scratch_shapes=[pltpu.VMEM((nt, tm, LANES), F32), pltpu.VMEM((SUBLANES, LANES), F32),
                        pltpu.VMEM((SUBLANES, LANES), F32)],
        compiler_params=_params(("arbitrary", "arbitrary")),
        name="route",
    )(idx_all, ltri, utri)


def _start_row_copies(tm, make_copy):
    def start(t, carry):
        for k in range(TOP_K):
            make_copy(t, k).start(priority=k % 2)
        return carry

    lax.fori_loop(0, tm, start, 0)


def _wait_row_copies(tm, make_copy):
    def wait(t, carry):
        for k in range(TOP_K):
            make_copy(t, k).wait()
        return carry

    lax.fori_loop(0, tm, wait, 0)


def _scatter_rows(dest_ref, x_ref, xs_hbm, sem):
    tm = x_ref.shape[0] // SUBLANES

    def make_copy(t, k):
        d = dest_ref[0, 0, t * TOP_K + k]
        return pltpu.make_async_copy(x_ref.at[_row_tile(t)], xs_hbm.at[_row_tile(d)], sem)

    _start_row_copies(tm, make_copy)
    _wait_row_copies(tm, make_copy)


def _dispatch_kernel(n_first, be_ref, nused_ref, dest_ref, xa_ref, xb_ref, xs_hbm, sem, zero_scr,
                     zero_sem):
    i = pl.program_id(0)

    @pl.when(i == 0)
    def _():
        zero_scr[...] = jnp.zeros_like(zero_scr)
        nblk = be_ref.shape[0]
        rows = zero_scr.shape[0]

        def guarded(action):
            def body(b, carry):
                nxt = be_ref[jnp.minimum(b + 1, nblk - 1)]
                may_have_gaps = jnp.logical_or(b >= nused_ref[0] - 1, be_ref[b] != nxt)

                @pl.when(may_have_gaps)
                def _():
                    dst = xs_hbm.at[pl.ds(pl.multiple_of(b * rows, rows), rows)]
                    action(pltpu.make_async_copy(zero_scr, dst, zero_sem))
                return carry
            return body

        lax.fori_loop(0, nblk, guarded(lambda c: c.start()), 0)
        lax.fori_loop(0, nblk, guarded(lambda c: c.wait()), 0)

    @pl.when(i < n_first)
    def _():
        _scatter_rows(dest_ref, xa_ref, xs_hbm, sem)

    @pl.when(i >= n_first)
    def _():
        _scatter_rows(dest_ref, xb_ref, xs_hbm, sem)


def _dispatch(dest3, x_first, x_second, block_e, nused, n_rows, tm_e):
    nt, _, width = dest3.shape
    tile_rows = width // TOP_K * SUBLANES
    n_first = x_first.shape[0] // tile_rows
    assert nt == n_first + x_second.shape[0] // tile_rows
    grid_spec = pltpu.PrefetchScalarGridSpec(
        num_scalar_prefetch=2,
        grid=(nt,),
        in_specs=[pl.BlockSpec((1, 1, width), lambda i, be, nu: (i, 0, 0), memory_space=pltpu.SMEM),
                  pl.BlockSpec((tile_rows, LANES), lambda i, be, nu: (jnp.minimum(i, n_first - 1), 0)),
                  pl.BlockSpec((tile_rows, LANES), lambda i, be, nu: (jnp.maximum(i - n_first, 0), 0))],
        out_specs=pl.BlockSpec(memory_space=pl.ANY),
        scratch_shapes=[pltpu.SemaphoreType.DMA(()), pltpu.VMEM((tm_e * SUBLANES, LANES), F32),
                        pltpu.SemaphoreType.DMA(())],
    )
    return pl.pallas_call(
        functools.partial(_dispatch_kernel, n_first),
        out_shape=jax.ShapeDtypeStruct((n_rows * SUBLANES, LANES), F32),
        grid_spec=grid_spec,
        compiler_params=_params(("arbitrary",)),
        name="dispatch",
    )(block_e, nused, dest3, x_first, x_second)


def _experts_kernel(be_ref, nused_ref, x_ref, wi_ref, bi_ref, wo_ref, bo_ref, o_ref,
                    wi_scr, wo_scr):
    i = pl.program_id(0)
    active = i < nused_ref[0]
    fresh = jnp.logical_or(i == 0, be_ref[i] != be_ref[jnp.maximum(i - 1, 0)])

    @pl.when(jnp.logical_and(active, fresh))
    def _():
        wi_scr[...] = wi_ref[0].astype(BF16)
        wo_scr[...] = wo_ref[0].astype(BF16)

    @pl.when(active)
    def _():
        rows = x_ref.shape[0] // SUBLANES
        x = _load_row_tiles(x_ref, rows)
        hdn = _dot(x.astype(BF16), wi_scr[...]) + bi_ref[0]
        gate = jnp.minimum(hdn[:, :D_FF], SWIGLU_LIMIT)
        up = jnp.clip(hdn[:, D_FF:], -SWIGLU_LIMIT, SWIGLU_LIMIT)
        act = (up + 1.0) * gate * _sigmoid(SWIGLU_ALPHA * gate)
        _store_row_tiles(o_ref, _dot(act.astype(BF16), wo_scr[...]) + bo_ref[0])

    @pl.when(jnp.logical_not(active))
    def _():
        o_ref[...] = jnp.zeros_like(o_ref)


def _experts(block_e, nused, xs, w_exp_in, b_exp_in, w_exp_out, b_exp_out, tm_e):
    n_rows = xs.shape[0] // SUBLANES
    tiled = pl.BlockSpec((tm_e * SUBLANES, LANES), lambda i, be, nu: (i, 0))
    grid_spec = pltpu.PrefetchScalarGridSpec(
        num_scalar_prefetch=2,
        grid=(n_rows // tm_e,),
        in_specs=[
            tiled,
            pl.BlockSpec((1, D_MODEL, 2 * D_FF), lambda i, be, nu: (be[i], 0, 0)),
            pl.BlockSpec((1, 1, 2 * D_FF), lambda i, be, nu: (be[i], 0, 0)),
            pl.BlockSpec((1, D_FF, D_MODEL), lambda i, be, nu: (be[i], 0, 0)),
            pl.BlockSpec((1, 1, D_MODEL), lambda i, be, nu: (be[i], 0, 0)),
        ],
        out_specs=tiled,
        scratch_shapes=[pltpu.VMEM((D_MODEL, 2 * D_FF), BF16), pltpu.VMEM((D_FF, D_MODEL), BF16)],
    )
    return pl.pallas_call(
        _experts_kernel,
        out_shape=jax.ShapeDtypeStruct(xs.shape, F32),
        grid_spec=grid_spec,
        compiler_params=_params(("arbitrary",)),
        name="experts",
    )(block_e, nused, xs, w_exp_in, b_exp_in, w_exp_out, b_exp_out)


def _final_kernel(n_tiles, dest_ref, dest_next_ref, h1_ref, gate_ref, p_ref, wpi_ref, wpg_ref,
                  ys_hbm, o_ref, rows_scr, sems):
    i = pl.program_id(0)
    tm = h1_ref.shape[0]
    slot = lax.rem(i, 2)

    def copies(idx_ref, half):
        def make_copy(t, k):
            d = idx_ref[0, 0, t * TOP_K + k]
            return pltpu.make_async_copy(ys_hbm.at[_row_tile(d)],
                                         rows_scr.at[half, k, _row_tile(t)], sems.at[half])
        return make_copy

    @pl.when(i == 0)
    def _():
        _start_row_copies(tm, copies(dest_ref, 0))

    if n_tiles > 1:
        @pl.when(i + 1 < n_tiles)
        def _():
            _start_row_copies(tm, copies(dest_next_ref, 1 - slot))

    _wait_row_copies(tm, copies(dest_ref, slot))
    gates = gate_ref[...]
    y = functools.reduce(jnp.add, [gates[:, k:k + 1] * _load_row_tiles(rows_scr, tm, (slot, k))
                                   for k in range(TOP_K)])
    h2 = h1_ref[...] + y
    ms = jnp.mean(h2 * h2, axis=-1, keepdims=True)
    hn = (h2 * lax.rsqrt(ms + EPS)).astype(BF16)
    gate = _sigmoid(_dot(hn, wpg_ref[...]))
    o_ref[...] = h2 + gate * _dot(p_ref[...].astype(BF16), wpi_ref[...])


def _final(dest3, h1, gates, p2d, ys, prm):
    nt, _, width = dest3.shape
    tm = width // TOP_K
    row = lambda i: (i, 0)
    const = lambda i: (0, 0)
    return pl.pallas_call(
        functools.partial(_final_kernel, nt),
        out_shape=jax.ShapeDtypeStruct(h1.shape, F32),
        grid=(nt,),
        in_specs=[pl.BlockSpec((1, 1, width), lambda i: (i, 0, 0), memory_space=pltpu.SMEM),
                  pl.BlockSpec((1, 1, width), lambda i: (jnp.minimum(i + 1, nt - 1), 0, 0),
                               memory_space=pltpu.SMEM),
                  pl.BlockSpec((tm, D_MODEL), row),
                  pl.BlockSpec((tm, LANES), row),
                  pl.BlockSpec((tm, PLE_DIM), row),
                  pl.BlockSpec((PLE_DIM, D_MODEL), const),
                  pl.BlockSpec((D_MODEL, D_MODEL), const),
                  pl.BlockSpec(memory_space=pl.ANY)],
        out_specs=pl.BlockSpec((tm, D_MODEL), row),
        scratch_shapes=[pltpu.VMEM((2, TOP_K, tm * SUBLANES, LANES), F32),
                        pltpu.SemaphoreType.DMA((2,))],
        compiler_params=_params(("arbitrary",)),
        name="final",
    )(dest3, dest3, h1, gates, p2d, prm["w_ple_in"], prm["w_ple_gate"], ys)


TOKEN_TILE = 256
PROJ_ROW_TILE = 1024
GLA_BLOCK = 256
EXPERT_TILE = 256
DA_QUERY_BLOCKS = 4
ROUTE_TILE_MAX = 2048
SAMPLE_PAD = SUBLANES


def kernel(x_prompt, x_sample, cache_da1_kv, cache_da2_kv, cache_da3_kv, state_gla, p_prompt, p_sample, norm_mix_g, w_in, w_gla_alpha, b_gla_alpha, gla_norm_g, w_gla_out, qk_norm_q, qk_norm_k, w_da_out, w_o, norm_moe_g, w_router, b_router, w_exp_in, b_exp_in, w_exp_out, b_exp_out, w_ple_in, w_ple_gate):
    depth = w_in.shape[0]
    assert depth == 1, "single-layer trunk"
    l = 0
    prm = _prep_params(norm_mix_g[l], w_in[l], w_gla_alpha[l], b_gla_alpha[l], gla_norm_g[l],
                       w_gla_out[l], qk_norm_q[l], qk_norm_k[l], w_da_out[l], w_o[l], norm_moe_g[l],
                       w_router[l], b_router[l], b_exp_in[l], b_exp_out[l], w_ple_in[l], w_ple_gate[l])
    n_b, seq, _ = x_prompt.shape
    d_b, t_new, _ = x_sample.shape
    caches = (cache_da1_kv[l], cache_da2_kv[l], cache_da3_kv[l])
    assert seq % (DA_SPAN * DA_DILATIONS[-1]) == 0 and seq % PROJ_ROW_TILE == 0
    assert t_new <= SAMPLE_PAD and (d_b * SAMPLE_PAD) % TOKEN_TILE == 0
    for g in range(N_DA_GROUPS):
        assert caches[g].shape[1] == DA_WINDOWS[g]

    hp = x_prompt.reshape(n_b * seq, D_MODEL)
    pp, la_p = _proj(hp, prm, PROJ_ROW_TILE, dilate=True, p_dtype=BF16)
    zero_state = jnp.zeros((n_b, GLA_HEADS, GLA_DK, GLA_DV), F32)
    og_p, state_p = _gla(pp, la_p, zero_state, prm["gla_norm_g"], n_b, seq, GLA_BLOCK, GLA_CHUNK)
    da_p = [_da_prompt(pp, g, n_b, seq, PROJ_ROW_TILE) for g in range(N_DA_GROUPS)]
    h1_p, xn_p, idx_p, gate_p = _mix(pp, og_p, [o for o, _ in da_p], [s for _, s in da_p], hp, prm,
                                     TOKEN_TILE, PROJ_ROW_TILE)

    pad_t = ((0, 0), (0, SAMPLE_PAD - t_new), (0, 0))
    hs = jnp.pad(x_sample, pad_t).reshape(d_b * SAMPLE_PAD, D_MODEL)
    ps, la_s = _proj(hs, prm, TOKEN_TILE, dilate=False, p_dtype=F32)
    og_s, state_s = _gla(ps, la_s, state_gla[l], prm["gla_norm_g"], d_b, SAMPLE_PAD, SAMPLE_PAD,
                         SAMPLE_PAD, t_valid=t_new)
    caches_t = [jnp.transpose(c, (0, 2, 3, 4, 1)) for c in caches]
    o_s, *new_caches_t = _da_sample(ps, caches_t, d_b, SAMPLE_PAD, t_new)
    h1_s, xn_s, idx_s, gate_s = _mix(ps, og_s, [o_s], [], hs, prm, TOKEN_TILE)

    n_p, n_s = hp.shape[0], hs.shape[0]
    n_rows = (n_p + n_s) * TOP_K + N_EXPERTS * EXPERT_TILE
    route_tile = max(c for c in range(TOKEN_TILE, ROUTE_TILE_MAX + 1, TOKEN_TILE)
                     if (n_p + n_s) % c == 0)
    dest, block_e, nused = _route(jnp.concatenate([idx_p, idx_s], axis=0), route_tile, EXPERT_TILE,
                                  n_rows // EXPERT_TILE)
    dest3 = dest[:, :TOP_K].reshape(-1, 1, TOKEN_TILE * TOP_K)
    dest3_p, dest3_s = dest3[:n_p // TOKEN_TILE], dest3[n_p // TOKEN_TILE:]
    block_e, nused = block_e[:n_rows // EXPERT_TILE, 0], nused[0, :1]
    xs = _dispatch(dest3, xn_p, xn_s, block_e, nused, n_rows, EXPERT_TILE)
    ys = _experts(block_e, nused, xs, w_exp_in[l], prm["b_exp_in"], w_exp_out[l], prm["b_exp_out"],
                  EXPERT_TILE)
    out_p = _final(dest3_p, h1_p, gate_p, p_prompt[l].reshape(n_p, PLE_DIM), ys, prm)
    out_s = _final(dest3_s, h1_s, gate_s, jnp.pad(p_sample[l], pad_t).reshape(n_s, PLE_DIM), ys, prm)

    y_prompt = out_p.reshape(n_b, seq, D_MODEL)
    y_sample = out_s.reshape(d_b, SAMPLE_PAD, D_MODEL)[:, :t_new]
    outs_p, outs_s = [], []
    for g in range(N_DA_GROUPS):
        win, dil = min(DA_WINDOWS[g], seq), DA_DILATIONS[g]
        ck = (COL_DA + 3 * g + 1) * PROJ_TN
        n_tiles = -(-win // PROJ_ROW_TILE)
        kv_p = pp.reshape(n_b, seq // PROJ_ROW_TILE, dil, PROJ_ROW_TILE // dil, PROJ_WIDTH)
        kv_p = kv_p[:, -n_tiles:, :, :, ck:ck + 2 * DA_WIDTH]
        kv_p = jnp.transpose(kv_p, (0, 1, 3, 2, 4)).reshape(n_b, n_tiles * PROJ_ROW_TILE, -1)
        outs_p.append(kv_p[:, -win:].astype(F32).reshape(1, n_b, win, 2, DA_HEADS, DA_HEAD_DIM))
        outs_s.append(jnp.transpose(new_caches_t[g], (0, 4, 1, 2, 3))[None])
    return (y_prompt, y_sample, state_p[None], state_s[None], outs_p[0], outs_s[0], outs_p[1],
            outs_s[1], outs_p[2], outs_s[2])
```

```python
import functools

import jax
import jax.numpy as jnp
from jax import lax
from jax.experimental import pallas as pl
from jax.experimental.pallas import tpu as pltpu

F32 = jnp.float32
BF16 = jnp.bfloat16
I32 = jnp.int32

D_MODEL = 1024
GLA_HEADS = 4
GLA_DK = 128
GLA_DV = 256
GLA_QK = GLA_HEADS * GLA_DK
GLA_V = GLA_HEADS * GLA_DV
GLA_RANK = 16
GLA_TAU = 16.0
GLA_CHUNK = 32
DA_WINDOWS = (128, 512, 2048)
DA_DILATIONS = (1, 4, 16)
DA_SPAN = 128
N_DA_GROUPS = 3
DA_HEADS = 8
DA_HEAD_DIM = 64
DA_WIDTH = DA_HEADS * DA_HEAD_DIM
N_EXPERTS = 32
TOP_K = 4
D_FF = D_MODEL
SWIGLU_LIMIT = 7.0
SWIGLU_ALPHA = 1.702
PLE_DIM = 256
EPS = 1e-6

LANES = 128
SUBLANES = 8
VMEM_LIMIT = 56 * 1024 * 1024

PROJ_TN = 512
COL_GQ, COL_GK, COL_GV, COL_GR, COL_GA, COL_GB, COL_DA = 0, 1, 2, 4, 6, 8, 10
PROJ_TILES = COL_DA + 3 * N_DA_GROUPS
PROJ_WIDTH = PROJ_TILES * PROJ_TN
NEG_BIG = -1e30

HIGHEST = lax.Precision.HIGHEST


def _dot(a, b):
    return jnp.dot(a, b, preferred_element_type=F32)


def _dot_nt(a, b):
    return lax.dot_general(a, b, (((1,), (1,)), ((), ())), preferred_element_type=F32)


def _dot_tn(a, b):
    return lax.dot_general(a, b, (((0,), (0,)), ((), ())), preferred_element_type=F32)


def _sigmoid(x):
    return 1.0 / (1.0 + jnp.exp(-x))


def _params(sem, vmem=VMEM_LIMIT):
    return pltpu.CompilerParams(dimension_semantics=sem, vmem_limit_bytes=vmem)


assert D_MODEL == SUBLANES * LANES


def _row_tile(r):
    return pl.ds(pl.multiple_of(r * SUBLANES, SUBLANES), SUBLANES)


def _store_row_tiles(ref, value, lead=()):
    rows = value.shape[0]
    for s in range(SUBLANES):
        ref[(*lead, pl.ds(s, rows, stride=SUBLANES), slice(None))] = value[:, s * LANES:(s + 1) * LANES]


def _load_row_tiles(ref, rows, lead=()):
    return jnp.concatenate(
        [ref[(*lead, pl.ds(s, rows, stride=SUBLANES), slice(None))] for s in range(SUBLANES)], axis=1)


def _proj_kernel(dilate, x_ref, g_ref, w_ref, wlr_ref, wal_ref, bal_ref, qkg_ref, bd_ref,
                 p_ref, la_ref, xn_scr, stage_scr):
    j = pl.program_id(1)
    tm = x_ref.shape[0]

    @pl.when(j == 0)
    def _():
        x = x_ref[...]
        ms = jnp.mean(x * x, axis=-1, keepdims=True)
        xn = (x * lax.rsqrt(ms + EPS) * g_ref[...]).astype(BF16)
        xn_scr[...] = xn
        glr = _dot(xn, wlr_ref[...])
        z = _dot(glr.astype(BF16), wal_ref[...]) + bal_ref[...]
        log_sig = jnp.minimum(z, 0.0) - jnp.log(1.0 + jnp.exp(-jnp.abs(z)))
        la_ref[...] = log_sig * (1.0 / GLA_TAU)

    acc = _dot(xn_scr[...], w_ref[...])
    is_qk = jnp.logical_and(j >= COL_DA, (j - COL_DA) % 3 != 2)
    group = (j - COL_DA) // 3
    staged = [jnp.logical_and(j >= COL_DA, group == g) if dilate and DA_DILATIONS[g] > 1 else None
              for g in range(N_DA_GROUPS)]
    any_staged = functools.reduce(jnp.logical_or, [s for s in staged if s is not None], False)

    def qk_normed():
        ssum = _dot((acc * acc).astype(BF16), bd_ref[...])
        return acc * lax.rsqrt(ssum * (1.0 / DA_HEAD_DIM) + EPS) * qkg_ref[...]

    def emit(value):
        if any_staged is False:
            p_ref[...] = value.astype(p_ref.dtype)
            return

        @pl.when(any_staged)
        def _():
            for c in range(PROJ_TN // LANES):
                stage_scr[c] = value[:, c * LANES:(c + 1) * LANES]

        @pl.when(jnp.logical_not(any_staged))
        def _():
            p_ref[...] = value.astype(p_ref.dtype)

    @pl.when(is_qk)
    def _():
        emit(qk_normed())

    @pl.when(jnp.logical_not(is_qk))
    def _():
        emit(acc)

    for g in range(N_DA_GROUPS):
        if staged[g] is None:
            continue
        dil = DA_DILATIONS[g]
        per = tm // dil

        @pl.when(staged[g])
        def _(dil=dil, per=per):
            for r in range(dil):
                for c in range(PROJ_TN // LANES):
                    p_ref[r * per:(r + 1) * per, c * LANES:(c + 1) * LANES] = (
                        stage_scr[c, pl.ds(r, per, stride=dil), :].astype(p_ref.dtype))


def _proj(x2d, prm, tm, dilate, p_dtype):
    m = x2d.shape[0]
    grid = (m // tm, PROJ_TILES)
    const = lambda i, j: (0, 0)
    return pl.pallas_call(
        functools.partial(_proj_kernel, dilate),
        out_shape=(jax.ShapeDtypeStruct((m, PROJ_WIDTH), p_dtype),
                   jax.ShapeDtypeStruct((m, GLA_QK), F32)),
        grid=grid,
        in_specs=[
            pl.BlockSpec((tm, D_MODEL), lambda i, j: (i, 0)),
            pl.BlockSpec((1, D_MODEL), const),
            pl.BlockSpec((D_MODEL, PROJ_TN), lambda i, j: (0, j)),
            pl.BlockSpec((D_MODEL, LANES), const),
            pl.BlockSpec((LANES, GLA_QK), const),
            pl.BlockSpec((1, GLA_QK), const),
            pl.BlockSpec((1, PROJ_TN), lambda i, j: (0, jnp.maximum(j - COL_DA, 0))),
            pl.BlockSpec((PROJ_TN, PROJ_TN), const),
        ],
        out_specs=(pl.BlockSpec((tm, PROJ_TN), lambda i, j: (i, j)),
                   pl.BlockSpec((tm, GLA_QK), lambda i, j: (i, 0))),
        scratch_shapes=[pltpu.VMEM((tm, D_MODEL), BF16),
                        pltpu.VMEM((PROJ_TN // LANES, tm, LANES), F32)],
        compiler_params=_params(("arbitrary", "arbitrary")),
        name="proj",
    )(x2d, prm["norm_mix_g"], prm["w_main"], prm["w_lr"], prm["w_alpha"], prm["b_alpha"],
      prm["qk_gain"], prm["head_ones"])


def _gla_kernel(chunk, t_valid, q_ref, k_ref, v_ref, r_ref, la_ref, s0_ref, g_ref, tri_ref,
                o_ref, s_ref, s_scr):
    t = pl.program_id(1)
    tb = q_ref.shape[0]
    n_chunks = tb // chunk
    row_ok = lax.broadcasted_iota(I32, (tb, GLA_DK), 0) < t_valid

    @pl.when(t == 0)
    def _():
        for h in range(GLA_HEADS):
            s_scr[h] = s0_ref[0, h].T

    tri = tri_ref[...]
    causal = tri > 0
    for h in range(GLA_HEADS):
        ks = slice(h * GLA_DK, (h + 1) * GLA_DK)
        vs = slice(h * GLA_DV, (h + 1) * GLA_DV)
        la = la_ref[:, ks]
        if t_valid < tb:
            la = jnp.where(row_ok, la, 0.0)
        la_hi = la.astype(BF16)
        la_lo = (la - la_hi.astype(F32)).astype(BF16)
        cum = _dot(tri, la_hi) + _dot(tri, la_lo)
        q = q_ref[:, ks].astype(F32) * (GLA_DK ** -0.5)
        k = k_ref[:, ks].astype(F32)
        v = v_ref[:, vs].astype(BF16)
        q_dec = (q * jnp.exp(cum)).astype(BF16)
        k_inv = (k * jnp.exp(-cum)).astype(BF16)
        att = jnp.where(causal, _dot_nt(q_dec, k_inv), 0.0)
        o_intra = _dot(att.astype(BF16), v)
        s_t = s_scr[h]
        for c in range(n_chunks):
            rows = slice(c * chunk, (c + 1) * chunk)
            tot = cum[(c + 1) * chunk - 1:(c + 1) * chunk, :]
            k_end = (k[rows] * jnp.exp(tot - cum[rows])).astype(BF16)
            o_c = o_intra[rows] + _dot_nt(q_dec[rows], s_t.astype(BF16))
            s_t = s_t * jnp.exp(tot) + _dot_tn(v[rows], k_end)
            ms = jnp.mean(o_c * o_c, axis=-1, keepdims=True)
            gate = r_ref[rows, vs].astype(F32)
            o_ref[rows, vs] = o_c * lax.rsqrt(ms + EPS) * g_ref[...] * (gate * _sigmoid(gate))
        s_scr[h] = s_t

    @pl.when(t == pl.num_programs(1) - 1)
    def _():
        for h in range(GLA_HEADS):
            s_ref[0, h] = s_scr[h].T


def _gla(p, la, s0, gla_norm_g, n_batch, t_len, tb, chunk, t_valid=None):
    nt = t_len // tb
    t_valid = tb if t_valid is None else t_valid
    r = jnp.arange(tb)
    same = (r[:, None] // chunk) == (r[None, :] // chunk)
    tri = (same & (r[:, None] >= r[None, :])).astype(BF16)
    row = lambda b, t: b * nt + t
    const = lambda b, t: (0, 0)
    return pl.pallas_call(
        functools.partial(_gla_kernel, chunk, t_valid),
        out_shape=(jax.ShapeDtypeStruct((n_batch * t_len, GLA_V), F32),
                   jax.ShapeDtypeStruct((n_batch, GLA_HEADS, GLA_DK, GLA_DV), F32)),
        grid=(n_batch, nt),
        in_specs=[
            pl.BlockSpec((tb, GLA_QK), lambda b, t: (row(b, t), COL_GQ)),
            pl.BlockSpec((tb, GLA_QK), lambda b, t: (row(b, t), COL_GK)),
            pl.BlockSpec((tb, GLA_V), lambda b, t: (row(b, t), COL_GV * PROJ_TN // GLA_V)),
            pl.BlockSpec((tb, GLA_V), lambda b, t: (row(b, t), COL_GR * PROJ_TN // GLA_V)),
            pl.BlockSpec((tb, GLA_QK), lambda b, t: (row(b, t), 0)),
            pl.BlockSpec((1, GLA_HEADS, GLA_DK, GLA_DV), lambda b, t: (b, 0, 0, 0)),
            pl.BlockSpec((1, GLA_DV), const),
            pl.BlockSpec((tb, tb), const),
        ],
        out_specs=(pl.BlockSpec((tb, GLA_V), lambda b, t: (row(b, t), 0)),
                   pl.BlockSpec((1, GLA_HEADS, GLA_DK, GLA_DV), lambda b, t: (b, 0, 0, 0))),
        scratch_shapes=[pltpu.VMEM((GLA_HEADS, GLA_DV, GLA_DK), F32)],
        compiler_params=_params(("arbitrary", "arbitrary")),
        name="gla",
    )(p, p, p, p, la, s0, gla_norm_g, tri)


def _prep_params(norm_mix_g, w_in, w_gla_alpha, b_gla_alpha, gla_norm_g, w_gla_out, qk_norm_q,
                 qk_norm_k, w_da_out, w_o, norm_moe_g, w_router, b_router, b_exp_in, b_exp_out,
                 w_ple_in, w_ple_gate):
    o_lr = GLA_QK * 2 + GLA_V * 2
    o_da = o_lr + GLA_RANK
    o_ga = o_da + 3 * N_DA_GROUPS * DA_WIDTH
    w_main = jnp.concatenate([w_in[:, :o_lr], w_in[:, o_ga:], w_in[:, o_da:o_ga]], axis=1).astype(BF16)
    w_lr = jnp.pad(w_in[:, o_lr:o_da], ((0, 0), (0, LANES - GLA_RANK))).astype(BF16)
    w_alpha = jnp.pad(w_gla_alpha, ((0, LANES - GLA_RANK), (0, 0))).astype(BF16)
    gains = []
    for g in range(N_DA_GROUPS):
        gains.append(jnp.tile(qk_norm_q[g], DA_HEADS) * (DA_HEAD_DIM ** -0.5))
        gains.append(jnp.tile(qk_norm_k[g], DA_HEADS))
        gains.append(jnp.ones((DA_WIDTH,), F32))
    qk_gain = jnp.concatenate(gains)[None, :]
    c = jnp.arange(PROJ_TN)
    head_ones = (c[:, None] // DA_HEAD_DIM == c[None, :] // DA_HEAD_DIM).astype(BF16)
    pad_e = LANES - N_EXPERTS
    w_r = jnp.pad(w_router, ((0, 0), (0, pad_e)))
    w_r_hi = w_r.astype(BF16)
    w_r_split = jnp.stack([w_r_hi, (w_r - w_r_hi.astype(F32)).astype(BF16)])
    return dict(
        norm_mix_g=norm_mix_g[None, :], w_main=w_main, w_lr=w_lr, w_alpha=w_alpha,
        b_alpha=b_gla_alpha[None, :], qk_gain=qk_gain, head_ones=head_ones,
        gla_norm_g=gla_norm_g[None, :], w_gla_out=w_gla_out.astype(BF16),
        w_da_out=w_da_out.astype(BF16), w_o=w_o.astype(BF16), norm_moe_g=norm_moe_g[None, :],
        w_router=w_r_split,
        b_router=jnp.pad(b_router, (0, pad_e), constant_values=NEG_BIG)[None, :],
        b_exp_in=b_exp_in[:, None, :], b_exp_out=b_exp_out[:, None, :],
        w_ple_in=w_ple_in.astype(BF16), w_ple_gate=w_ple_gate.astype(BF16),
    )


def _da_prompt_kernel(q_ref, kc_ref, kp_ref, vc_ref, vp_ref, bias_ref, o_ref, lse_ref):
    band = bias_ref[1]
    first = jnp.where(pl.program_id(2) == 0, bias_ref[0], band)
    low = lax.broadcasted_iota(I32, (DA_SPAN, LANES), 1) < DA_HEAD_DIM
    pieces = q_ref.shape[1:4:2]
    n_q = pieces[0] * pieces[1] // DA_SPAN

    def rows(ref, cs):
        return ref[0, :, 0, :, cs].reshape(-1, LANES)

    for a in range(DA_WIDTH // LANES):
        cs = slice(a * LANES, (a + 1) * LANES)
        q_all = rows(q_ref, cs)
        k_all = jnp.concatenate([rows(kp_ref, cs), rows(kc_ref, cs)], axis=0).astype(BF16)
        v_all = jnp.concatenate([rows(vp_ref, cs), rows(vc_ref, cs)], axis=0).astype(BF16)
        outs, lses = [], []
        for j in range(n_q):
            q2 = q_all[j * DA_SPAN:(j + 1) * DA_SPAN]
            k2 = k_all[j * DA_SPAN:(j + 2) * DA_SPAN]
            v2 = v_all[j * DA_SPAN:(j + 2) * DA_SPAN]
            res = []
            for sel in (low, jnp.logical_not(low)):
                s = _dot_nt(jnp.where(sel, q2, 0.0).astype(BF16), k2) + (first if j == 0 else band)
                m = jnp.max(s, axis=-1, keepdims=True)
                p = jnp.exp(s - m)
                l = jnp.sum(p, axis=-1, keepdims=True)
                res.append((_dot(p.astype(BF16), v2) / l, m + jnp.log(l)))
            outs.append(jnp.where(low, res[0][0], res[1][0]))
            lses.append(jnp.where(low, res[0][1], res[1][1]))
        o_ref[0, :, 0, :, cs] = jnp.concatenate(outs, axis=0).reshape(*pieces, LANES)
        lse_ref[0, :, 0, :, cs] = jnp.concatenate(lses, axis=0).reshape(*pieces, LANES)


def _da_bias():
    qi = jnp.arange(DA_SPAN)[:, None]
    kj = jnp.arange(2 * DA_SPAN)[None, :]
    d = kj - qi
    band = (d >= 0) & (d <= DA_SPAN)
    first = band & (kj >= DA_SPAN)
    return jnp.where(jnp.stack([first, band]), 0.0, NEG_BIG).astype(F32)


def _da_prompt(p, g, n_batch, seq, tile):
    dil = DA_DILATIONS[g]
    per = tile // dil
    sub_len = seq // dil
    q_blocks = min(DA_QUERY_BLOCKS, sub_len // DA_SPAN)
    p5 = p.reshape(n_batch, seq // tile, dil, per, PROJ_WIDTH)
    cq, ck, cv = (COL_DA + 3 * g + i for i in range(3))

    def spec(c, rows, block_of_step):
        if per >= rows:
            assert per % rows == 0
            blk = (1, 1, 1, rows, DA_WIDTH)
            place = lambda m: (m // (per // rows), m % (per // rows))
        else:
            assert rows % per == 0 and per % SUBLANES == 0
            blk = (1, rows // per, 1, per, DA_WIDTH)
            place = lambda m: (m, 0)

        def index(b, r, n):
            tile_i, row_i = place(block_of_step(n))
            return (b, tile_i, r, row_i, c)
        return pl.BlockSpec(blk, index)

    cur = lambda c: spec(c, q_blocks * DA_SPAN, lambda n: n)
    prev = lambda c: spec(c, DA_SPAN, lambda n: jnp.maximum(q_blocks * n - 1, 0))
    out_sds = jax.ShapeDtypeStruct((n_batch, seq // tile, dil, per, DA_WIDTH), F32)
    return pl.pallas_call(
        _da_prompt_kernel,
        out_shape=(out_sds, out_sds),
        grid=(n_batch, dil, sub_len // (q_blocks * DA_SPAN)),
        in_specs=[cur(cq), cur(ck), prev(ck), cur(cv), prev(cv),
                  pl.BlockSpec((2, DA_SPAN, 2 * DA_SPAN), lambda b, r, n: (0, 0, 0))],
        out_specs=(cur(0), cur(0)),
        compiler_params=_params(("arbitrary", "arbitrary", "arbitrary")),
        name=f"da_prompt{g}",
    )(p5, p5, p5, p5, p5, _da_bias())


def _mix_kernel(dils, *refs):
    og_ref = refs[0]
    n_groups = len(dils) if dils else 1
    n_lse = n_groups if dils else 0
    o_refs = refs[1:1 + n_groups]
    l_refs = refs[1 + n_groups:1 + n_groups + n_lse]
    (ga_ref, gb_ref, h_ref, wga_ref, wda_ref, wo_ref, gm_ref, wr_ref, br_ref,
     h1_ref, xn_ref, idx_ref, gate_ref) = refs[1 + n_groups + n_lse:14 + n_groups + n_lse]
    scratch = iter(refs[14 + n_groups + n_lse:])

    def token_major(ref, dil):
        if dil == 1:
            return ref[0, 0]
        scr = next(scratch)
        per = ref.shape[2]
        for r in range(dil):
            for c in range(DA_WIDTH // LANES):
                scr[c, pl.ds(r, per, stride=dil), :] = ref[0, r, :, c * LANES:(c + 1) * LANES]
        return jnp.concatenate([scr[c] for c in range(DA_WIDTH // LANES)], axis=1)

    if not dils:
        o_da = o_refs[0][...]
    else:
        outs = [token_major(r, d) for r, d in zip(o_refs, dils)]
        lses = [token_major(r, d) for r, d in zip(l_refs, dils)]
        m = functools.reduce(jnp.maximum, lses)
        es = [jnp.exp(l - m) for l in lses]
        num = functools.reduce(jnp.add, [e * o for e, o in zip(es, outs)])
        o_da = num / functools.reduce(jnp.add, es)
    y_a = _dot(og_ref[...].astype(BF16), wga_ref[...])
    y_b = _dot(o_da.astype(BF16), wda_ref[...])
    y = _sigmoid(ga_ref[...].astype(F32)) * y_a + _sigmoid(gb_ref[...].astype(F32)) * y_b
    h1 = h_ref[...] + _dot(y.astype(BF16), wo_ref[...])
    h1_ref[...] = h1
    ms = jnp.mean(h1 * h1, axis=-1, keepdims=True)
    xn = h1 * lax.rsqrt(ms + EPS) * gm_ref[...]
    _store_row_tiles(xn_ref, xn)
    xn_hi = xn.astype(BF16)
    xn_lo = (xn - xn_hi.astype(F32)).astype(BF16)
    logits = (_dot(xn_hi, wr_ref[0]) + _dot(xn_lo, wr_ref[0]) + _dot(xn_hi, wr_ref[1])) + br_ref[...]
    lane = lax.broadcasted_iota(I32, logits.shape, 1)
    lane_f = lane.astype(F32)
    work = logits
    idx_out = jnp.zeros(logits.shape, I32)
    val_out = jnp.zeros(logits.shape, F32)
    vals = []
    for k in range(TOP_K):
        top = jnp.max(work, axis=-1, keepdims=True)
        pick = jnp.min(jnp.where(work == top, lane_f, float(LANES)), axis=-1, keepdims=True)
        pick = pick.astype(I32)
        vals.append(top)
        idx_out = jnp.where(lane == k, pick, idx_out)
        work = jnp.where(lane == pick, -3e38, work)
    es = [jnp.exp(v - vals[0]) for v in vals]
    den = functools.reduce(jnp.add, es)
    for k in range(TOP_K):
        val_out = jnp.where(lane == k, es[k] / den, val_out)
    idx_ref[...] = idx_out
    gate_ref[...] = val_out


def _mix(p, o_gla, o_das, lses, h2d, prm, tm, tile=None):
    m = h2d.shape[0]
    row = lambda i: (i, 0)
    const = lambda i: (0, 0)
    wide = pl.BlockSpec((tm, D_MODEL), row)
    small = pl.BlockSpec((tm, LANES), row)
    if lses:
        dils = DA_DILATIONS
        sub = tile // tm
        da_in, da_specs, scratch = [], [], []
        for arr, dil in list(zip(o_das, dils)) + list(zip(lses, dils)):
            assert (tm // dil) % SUBLANES == 0
            da_in.append(arr.reshape(-1, dil, tile // dil, DA_WIDTH))
            da_specs.append(pl.BlockSpec((1, dil, tm // dil, DA_WIDTH),
                                         lambda i: (i // sub, 0, i % sub, 0)))
            if dil > 1:
                scratch.append(pltpu.VMEM((DA_WIDTH // LANES, tm, LANES), F32))
    else:
        dils, da_in, scratch = None, list(o_das), []
        da_specs = [pl.BlockSpec((tm, DA_WIDTH), row)]
    return pl.pallas_call(
        functools.partial(_mix_kernel, dils),
        out_shape=(jax.ShapeDtypeStruct((m, D_MODEL), F32),
                   jax.ShapeDtypeStruct((m * SUBLANES, LANES), F32),
                   jax.ShapeDtypeStruct((m, LANES), I32), jax.ShapeDtypeStruct((m, LANES), F32)),
        grid=(m // tm,),
        scratch_shapes=scratch,
        in_specs=[wide] + da_specs + [
            pl.BlockSpec((tm, D_MODEL), lambda i: (i, COL_GA * PROJ_TN // D_MODEL)),
            pl.BlockSpec((tm, D_MODEL), lambda i: (i, COL_GB * PROJ_TN // D_MODEL)),
            wide,
            pl.BlockSpec((GLA_V, D_MODEL), const),
            pl.BlockSpec((DA_WIDTH, D_MODEL), const),
            pl.BlockSpec((D_MODEL, D_MODEL), const),
            pl.BlockSpec((1, D_MODEL), const),
            pl.BlockSpec((2, D_MODEL, LANES), lambda i: (0, 0, 0)),
            pl.BlockSpec((1, LANES), const),
        ],
        out_specs=(wide, pl.BlockSpec((tm * SUBLANES, LANES), row), small, small),
        compiler_params=_params(("arbitrary",)),
        name="mix",
    )(o_gla, *da_in, p, p, h2d, prm["w_gla_out"], prm["w_da_out"], prm["w_o"],
      prm["norm_moe_g"], prm["w_router"], prm["b_router"])


def _da_sample_kernel(t_new, *refs):
    qkv = [refs[3 * g:3 * g + 3] for g in range(N_DA_GROUPS)]
    c_refs = refs[9:12]
    sel_ref = refs[12]
    o_ref = refs[13]
    new_refs = refs[14:17]
    t_pad = o_ref.shape[0]
    q_row = lax.broadcasted_iota(I32, (t_pad, t_pad), 0)
    k_row = lax.broadcasted_iota(I32, (t_pad, t_pad), 1)
    lane = lax.broadcasted_iota(I32, (DA_HEAD_DIM, LANES), 1)
    heads = []
    for hh in range(LANES // DA_HEAD_DIM):
        hs = slice(hh * DA_HEAD_DIM, (hh + 1) * DA_HEAD_DIM)
        lses, accs, dens = [], [], []
        for g in range(N_DA_GROUPS):
            dil, win = DA_DILATIONS[g], DA_WINDOWS[g]
            q_ref, k_ref, v_ref = qkv[g]
            q = q_ref[:, hs].astype(BF16)
            k_new = k_ref[:, hs]
            v_new = v_ref[:, hs]
            kt = c_refs[g][0, 0, hh]
            vt = c_refs[g][0, 1, hh]
            t_idx = lax.broadcasted_iota(I32, (t_pad, win), 0)
            off = lax.broadcasted_iota(I32, (t_pad, win), 1) - t_idx
            ok = jnp.logical_and(off >= 0, jnp.bitwise_and(off, dil - 1) == 0)
            s = jnp.where(ok, _dot(q, kt.astype(BF16)), NEG_BIG)
            back = q_row - k_row
            ok_new = jnp.logical_and(jnp.logical_and(back >= 0, jnp.bitwise_and(back, dil - 1) == 0),
                                     k_row < t_new)
            s_new = jnp.where(ok_new, _dot_nt(q, k_new.astype(BF16)), NEG_BIG)
            m = jnp.maximum(jnp.max(s, axis=-1, keepdims=True), jnp.max(s_new, axis=-1, keepdims=True))
            p = jnp.exp(s - m)
            p_new = jnp.exp(s_new - m)
            dens.append(jnp.sum(p, axis=-1, keepdims=True) + jnp.sum(p_new, axis=-1, keepdims=True))
            accs.append(_dot_nt(p.astype(BF16), vt.astype(BF16))
                        + _dot(p_new.astype(BF16), v_new.astype(BF16)))
            lses.append(m + jnp.log(dens[-1]))
            for kv, (old, new) in enumerate(((kt, k_new), (vt, v_new))):
                moved = pltpu.roll(old, win - t_new, axis=1)
                tail = lax.dot_general(new, sel_ref[...], (((0,), (0,)), ((), ())),
                                       preferred_element_type=F32, precision=HIGHEST)
                tail = jnp.where(lane >= LANES - t_new, tail, moved[:, win - LANES:])
                if win > LANES:
                    new_refs[g][0, kv, hh, :, :win - LANES] = moved[:, :win - LANES]
                new_refs[g][0, kv, hh, :, win - LANES:] = tail
        top = functools.reduce(jnp.maximum, lses)
        ws = [jnp.exp(l - top) for l in lses]
        wsum = functools.reduce(jnp.add, ws)
        heads.append(functools.reduce(
            jnp.add, [accs[g] * (ws[g] / (wsum * dens[g])) for g in range(N_DA_GROUPS)]))
    o_ref[...] = jnp.concatenate(heads, axis=1)


def _da_sample(p_s, caches_t, n_batch, t_pad, t_new):
    pair = LANES // DA_HEAD_DIM
    t = jnp.arange(t_pad)[:, None]
    sel = ((jnp.arange(LANES)[None, :] == LANES - t_new + t) & (t < t_new)).astype(F32)
    per_tile = PROJ_TN // LANES
    qkv_specs = [pl.BlockSpec((t_pad, LANES), lambda b, a, col=COL_DA + i: (b, col * per_tile + a))
                 for i in range(3 * N_DA_GROUPS)]
    c_specs = [pl.BlockSpec((1, 2, pair, DA_HEAD_DIM, DA_WINDOWS[g]), lambda b, a: (b, 0, a, 0, 0))
               for g in range(N_DA_GROUPS)]
    return pl.pallas_call(
        functools.partial(_da_sample_kernel, t_new),
        out_shape=(jax.ShapeDtypeStruct((n_batch * t_pad, DA_WIDTH), F32),
                   *[jax.ShapeDtypeStruct(c.shape, F32) for c in caches_t]),
        grid=(n_batch, DA_HEADS // pair),
        in_specs=qkv_specs + c_specs + [pl.BlockSpec((t_pad, LANES), lambda b, a: (0, 0))],
        out_specs=(pl.BlockSpec((t_pad, LANES), lambda b, a: (b, a)), *c_specs),
        compiler_params=_params(("arbitrary", "arbitrary")),
        name="da_sample",
    )(*([p_s] * (3 * N_DA_GROUPS)), *caches_t, sel)


def _route_kernel(tm_e, idx_ref, ltri_ref, utri_ref, dest_ref, be_ref, nused_ref,
                  rank_scr, carry_scr, pstart_scr):
    ph = pl.program_id(0)
    i = pl.program_id(1)
    tm = idx_ref.shape[0]
    lane = lax.broadcasted_iota(I32, (tm, LANES), 1)
    idx = idx_ref[...]
    hots = [lane == idx[:, k:k + 1] for k in range(TOP_K)]

    @pl.when(ph == 0)
    def _():
        @pl.when(i == 0)
        def _():
            carry_scr[...] = jnp.zeros_like(carry_scr)

        cnt = functools.reduce(jnp.add, [h.astype(F32) for h in hots])
        before = _dot(ltri_ref[...], cnt.astype(BF16)) + carry_scr[0:1, :]
        rank = jnp.zeros((tm, LANES), F32)
        for k in range(TOP_K):
            r_k = jnp.sum(jnp.where(hots[k], before, 0.0), axis=-1, keepdims=True)
            rank = jnp.where(lane == k, r_k, rank)
        rank_scr[i] = rank
        carry_scr[...] = carry_scr[...] + jnp.sum(cnt, axis=0, keepdims=True)

    @pl.when(ph == 1)
    def _():
        @pl.when(i == 0)
        def _():
            counts = carry_scr[...]
            padded = jnp.floor((counts + (tm_e - 1)) * (1.0 / tm_e)) * tm_e
            pend = jnp.dot(padded, utri_ref[...], preferred_element_type=F32, precision=HIGHEST)
            pstart_scr[...] = pend - padded
            nblk = be_ref.shape[0]
            start = lax.broadcasted_iota(I32, (nblk, LANES), 0).astype(F32) * tm_e
            is_e = lax.broadcasted_iota(I32, (nblk, LANES), 1) < N_EXPERTS
            done = jnp.where(jnp.logical_and(pend[0:1, :] <= start, is_e), 1.0, 0.0)
            be = jnp.minimum(jnp.sum(done, axis=-1, keepdims=True), N_EXPERTS - 1.0)
            be_ref[...] = jnp.broadcast_to(be, (nblk, LANES)).astype(I32)
            total = pend[:, N_EXPERTS - 1:N_EXPERTS] * (1.0 / tm_e)
            nused_ref[...] = jnp.broadcast_to(total, nused_ref.shape).astype(I32)

        dest = rank_scr[i]
        for k in range(TOP_K):
            d_k = jnp.sum(jnp.where(hots[k], pstart_scr[0:1, :], 0.0), axis=-1, keepdims=True)
            dest = dest + jnp.where(lane == k, d_k, 0.0)
        dest_ref[...] = dest.astype(I32)


def _route(idx_all, tm, tm_e, nblk):
    n = idx_all.shape[0]
    nt = n // tm
    r = jnp.arange(tm)
    ltri = (r[:, None] > r[None, :]).astype(BF16)
    c = jnp.arange(LANES)
    utri = ((c[:, None] <= c[None, :]) & (c[:, None] < N_EXPERTS)).astype(F32)
    nblk_pad = -(-nblk // SUBLANES) * SUBLANES
    return pl.pallas_call(
        functools.partial(_route_kernel, tm_e),
        out_shape=(jax.ShapeDtypeStruct((n, LANES), I32),
                   jax.ShapeDtypeStruct((nblk_pad, LANES), I32),
                   jax.ShapeDtypeStruct((SUBLANES, LANES), I32)),
        grid=(2, nt),
        in_specs=[pl.BlockSpec((tm, LANES), lambda ph, i: (i, 0)),
                  pl.BlockSpec((tm, tm), lambda ph, i: (0, 0)),
                  pl.BlockSpec((LANES, LANES), lambda ph, i: (0, 0))],
        out_specs=(pl.BlockSpec((tm, LANES), lambda ph, i: (i * ph, 0)),
                   pl.BlockSpec((nblk_pad, LANES), lambda ph, i: (0, 0)),
                   pl.BlockSpec((SUBLANES, LANES), lambda ph, i: (0, 0))),
        scratch_shapes=[pltpu.VMEM((nt, tm, LANES), F32), pltpu.VMEM((SUBLANES, LANES), F32),
                        pltpu.VMEM((SUBLANES, LANES), F32)],
        compiler_params=_params(("arbitrary", "arbitrary")),
        name="route",
    )(idx_all, ltri, utri)


def _start_row_copies(tm, make_copy):
    def start(t, carry):
        for k in range(TOP_K):
            make_copy(t, k).start(priority=k % 2)
        return carry

    lax.fori_loop(0, tm, start, 0)


def _wait_row_copies(tm, make_copy):
    def wait(t, carry):
        for k in range(TOP_K):
            make_copy(t, k).wait()
        return carry

    lax.fori_loop(0, tm, wait, 0)


def _scatter_rows(dest_ref, x_ref, xs_hbm, sem):
    tm = x_ref.shape[0] // SUBLANES

    def make_copy(t, k):
        d = dest_ref[0, 0, t * TOP_K + k]
        return pltpu.make_async_copy(x_ref.at[_row_tile(t)], xs_hbm.at[_row_tile(d)], sem)

    _start_row_copies(tm, make_copy)
    _wait_row_copies(tm, make_copy)


def _dispatch_kernel(n_first, be_ref, nused_ref, dest_ref, xa_ref, xb_ref, xs_hbm, sem, zero_scr,
                     zero_sem):
    i = pl.program_id(0)

    @pl.when(i == 0)
    def _():
        zero_scr[...] = jnp.zeros_like(zero_scr)
        nblk = be_ref.shape[0]
        rows = zero_scr.shape[0]

        def guarded(action):
            def body(b, carry):
                nxt = be_ref[jnp.minimum(b + 1, nblk - 1)]
                may_have_gaps = jnp.logical_or(b >= nused_ref[0] - 1, be_ref[b] != nxt)

                @pl.when(may_have_gaps)
                def _():
                    dst = xs_hbm.at[pl.ds(pl.multiple_of(b * rows, rows), rows)]
                    action(pltpu.make_async_copy(zero_scr, dst, zero_sem))
                return carry
            return body

        lax.fori_loop(0, nblk, guarded(lambda c: c.start()), 0)
        lax.fori_loop(0, nblk, guarded(lambda c: c.wait()), 0)

    @pl.when(i < n_first)
    def _():
        _scatter_rows(dest_ref, xa_ref, xs_hbm, sem)

    @pl.when(i >= n_first)
    def _():
        _scatter_rows(dest_ref, xb_ref, xs_hbm, sem)


def _dispatch(dest3, x_first, x_second, block_e, nused, n_rows, tm_e):
    nt, _, width = dest3.shape
    tile_rows = width // TOP_K * SUBLANES
    n_first = x_first.shape[0] // tile_rows
    assert nt == n_first + x_second.shape[0] // tile_rows
    grid_spec = pltpu.PrefetchScalarGridSpec(
        num_scalar_prefetch=2,
        grid=(nt,),
        in_specs=[pl.BlockSpec((1, 1, width), lambda i, be, nu: (i, 0, 0), memory_space=pltpu.SMEM),
                  pl.BlockSpec((tile_rows, LANES), lambda i, be, nu: (jnp.minimum(i, n_first - 1), 0)),
                  pl.BlockSpec((tile_rows, LANES), lambda i, be, nu: (jnp.maximum(i - n_first, 0), 0))],
        out_specs=pl.BlockSpec(memory_space=pl.ANY),
        scratch_shapes=[pltpu.SemaphoreType.DMA(()), pltpu.VMEM((tm_e * SUBLANES, LANES), F32),
                        pltpu.SemaphoreType.DMA(())],
    )
    return pl.pallas_call(
        functools.partial(_dispatch_kernel, n_first),
        out_shape=jax.ShapeDtypeStruct((n_rows * SUBLANES, LANES), F32),
        grid_spec=grid_spec,
        compiler_params=_params(("arbitrary",)),
        name="dispatch",
    )(block_e, nused, dest3, x_first, x_second)


def _experts_kernel(be_ref, nused_ref, x_ref, wi_ref, bi_ref, wo_ref, bo_ref, o_ref,
                    wi_scr, wo_scr):
    i = pl.program_id(0)
    active = i < nused_ref[0]
    fresh = jnp.logical_or(i == 0, be_ref[i] != be_ref[jnp.maximum(i - 1, 0)])

    @pl.when(jnp.logical_and(active, fresh))
    def _():
        wi_scr[...] = wi_ref[0].astype(BF16)
        wo_scr[...] = wo_ref[0].astype(BF16)

    @pl.when(active)
    def _():
        rows = x_ref.shape[0] // SUBLANES
        x = _load_row_tiles(x_ref, rows)
        hdn = _dot(x.astype(BF16), wi_scr[...]) + bi_ref[0]
        gate = jnp.minimum(hdn[:, :D_FF], SWIGLU_LIMIT)
        up = jnp.clip(hdn[:, D_FF:], -SWIGLU_LIMIT, SWIGLU_LIMIT)
        act = (up + 1.0) * gate * _sigmoid(SWIGLU_ALPHA * gate)
        _store_row_tiles(o_ref, _dot(act.astype(BF16), wo_scr[...]) + bo_ref[0])

    @pl.when(jnp.logical_not(active))
    def _():
        o_ref[...] = jnp.zeros_like(o_ref)


def _experts(block_e, nused, xs, w_exp_in, b_exp_in, w_exp_out, b_exp_out, tm_e):
    n_rows = xs.shape[0] // SUBLANES
    tiled = pl.BlockSpec((tm_e * SUBLANES, LANES), lambda i, be, nu: (i, 0))
    grid_spec = pltpu.PrefetchScalarGridSpec(
        num_scalar_prefetch=2,
        grid=(n_rows // tm_e,),
        in_specs=[
            tiled,
            pl.BlockSpec((1, D_MODEL, 2 * D_FF), lambda i, be, nu: (be[i], 0, 0)),
            pl.BlockSpec((1, 1, 2 * D_FF), lambda i, be, nu: (be[i], 0, 0)),
            pl.BlockSpec((1, D_FF, D_MODEL), lambda i, be, nu: (be[i], 0, 0)),
            pl.BlockSpec((1, 1, D_MODEL), lambda i, be, nu: (be[i], 0, 0)),
        ],
        out_specs=tiled,
        scratch_shapes=[pltpu.VMEM((D_MODEL, 2 * D_FF), BF16), pltpu.VMEM((D_FF, D_MODEL), BF16)],
    )
    return pl.pallas_call(
        _experts_kernel,
        out_shape=jax.ShapeDtypeStruct(xs.shape, F32),
        grid_spec=grid_spec,
        compiler_params=_params(("arbitrary",)),
        name="experts",
    )(block_e, nused, xs, w_exp_in, b_exp_in, w_exp_out, b_exp_out)


def _final_kernel(n_tiles, dest_ref, dest_next_ref, h1_ref, gate_ref, p_ref, wpi_ref, wpg_ref,
                  ys_hbm, o_ref, rows_scr, sems):
    i = pl.program_id(0)
    tm = h1_ref.shape[0]
    slot = lax.rem(i, 2)

    def copies(idx_ref, half):
        def make_copy(t, k):
            d = idx_ref[0, 0, t * TOP_K + k]
            return pltpu.make_async_copy(ys_hbm.at[_row_tile(d)],
                                         rows_scr.at[half, k, _row_tile(t)], sems.at[half])
        return make_copy

    @pl.when(i == 0)
    def _():
        _start_row_copies(tm, copies(dest_ref, 0))

    if n_tiles > 1:
        @pl.when(i + 1 < n_tiles)
        def _():
            _start_row_copies(tm, copies(dest_next_ref, 1 - slot))

    _wait_row_copies(tm, copies(dest_ref, slot))
    gates = gate_ref[...]
    y = functools.reduce(jnp.add, [gates[:, k:k + 1] * _load_row_tiles(rows_scr, tm, (slot, k))
                                   for k in range(TOP_K)])
    h2 = h1_ref[...] + y
    ms = jnp.mean(h2 * h2, axis=-1, keepdims=True)
    hn = (h2 * lax.rsqrt(ms + EPS)).astype(BF16)
    gate = _sigmoid(_dot(hn, wpg_ref[...]))
    o_ref[...] = h2 + gate * _dot(p_ref[...].astype(BF16), wpi_ref[...])


def _final(dest3, h1, gates, p2d, ys, prm):
    nt, _, width = dest3.shape
    tm = width // TOP_K
    row = lambda i: (i, 0)
    const = lambda i: (0, 0)
    return pl.pallas_call(
        functools.partial(_final_kernel, nt),
        out_shape=jax.ShapeDtypeStruct(h1.shape, F32),
        grid=(nt,),
        in_specs=[pl.BlockSpec((1, 1, width), lambda i: (i, 0, 0), memory_space=pltpu.SMEM),
                  pl.BlockSpec((1, 1, width), lambda i: (jnp.minimum(i + 1, nt - 1), 0, 0),
                               memory_space=pltpu.SMEM),
                  pl.BlockSpec((tm, D_MODEL), row),
                  pl.BlockSpec((tm, LANES), row),
                  pl.BlockSpec((tm, PLE_DIM), row),
                  pl.BlockSpec((PLE_DIM, D_MODEL), const),
                  pl.BlockSpec((D_MODEL, D_MODEL), const),
                  pl.BlockSpec(memory_space=pl.ANY)],
        out_specs=pl.BlockSpec((tm, D_MODEL), row),
        scratch_shapes=[pltpu.VMEM((2, TOP_K, tm * SUBLANES, LANES), F32),
                        pltpu.SemaphoreType.DMA((2,))],
        compiler_params=_params(("arbitrary",)),
        name="final",
    )(dest3, dest3, h1, gates, p2d, prm["w_ple_in"], prm["w_ple_gate"], ys)


TOKEN_TILE = 256
PROJ_ROW_TILE = 2048
GLA_BLOCK = 256
EXPERT_TILE = 256
DA_QUERY_BLOCKS = 8
ROUTE_TILE_MAX = 2048
SAMPLE_PAD = SUBLANES


def kernel(x_prompt, x_sample, cache_da1_kv, cache_da2_kv, cache_da3_kv, state_gla, p_prompt, p_sample, norm_mix_g, w_in, w_gla_alpha, b_gla_alpha, gla_norm_g, w_gla_out, qk_norm_q, qk_norm_k, w_da_out, w_o, norm_moe_g, w_router, b_router, w_exp_in, b_exp_in, w_exp_out, b_exp_out, w_ple_in, w_ple_gate):
    depth = w_in.shape[0]
    assert depth == 1, "single-layer trunk"
    l = 0
    prm = _prep_params(norm_mix_g[l], w_in[l], w_gla_alpha[l], b_gla_alpha[l], gla_norm_g[l],
                       w_gla_out[l], qk_norm_q[l], qk_norm_k[l], w_da_out[l], w_o[l], norm_moe_g[l],
                       w_router[l], b_router[l], b_exp_in[l], b_exp_out[l], w_ple_in[l], w_ple_gate[l])
    n_b, seq, _ = x_prompt.shape
    d_b, t_new, _ = x_sample.shape
    caches = (cache_da1_kv[l], cache_da2_kv[l], cache_da3_kv[l])
    assert seq % (DA_SPAN * DA_DILATIONS[-1]) == 0 and seq % PROJ_ROW_TILE == 0
    assert t_new <= SAMPLE_PAD and (d_b * SAMPLE_PAD) % TOKEN_TILE == 0
    for g in range(N_DA_GROUPS):
        assert caches[g].shape[1] == DA_WINDOWS[g]

    hp = x_prompt.reshape(n_b * seq, D_MODEL)
    pp, la_p = _proj(hp, prm, PROJ_ROW_TILE, dilate=True, p_dtype=BF16)
    zero_state = jnp.zeros((n_b, GLA_HEADS, GLA_DK, GLA_DV), F32)
    og_p, state_p = _gla(pp, la_p, zero_state, prm["gla_norm_g"], n_b, seq, GLA_BLOCK, GLA_CHUNK)
    da_p = [_da_prompt(pp, g, n_b, seq, PROJ_ROW_TILE) for g in range(N_DA_GROUPS)]
    h1_p, xn_p, idx_p, gate_p = _mix(pp, og_p, [o for o, _ in da_p], [s for _, s in da_p], hp, prm,
                                     TOKEN_TILE, PROJ_ROW_TILE)

    pad_t = ((0, 0), (0, SAMPLE_PAD - t_new), (0, 0))
    hs = jnp.pad(x_sample, pad_t).reshape(d_b * SAMPLE_PAD, D_MODEL)
    ps, la_s = _proj(hs, prm, TOKEN_TILE, dilate=False, p_dtype=F32)
    og_s, state_s = _gla(ps, la_s, state_gla[l], prm["gla_norm_g"], d_b, SAMPLE_PAD, SAMPLE_PAD,
                         SAMPLE_PAD, t_valid=t_new)
    caches_t = [jnp.transpose(c, (0, 2, 3, 4, 1)) for c in caches]
    o_s, *new_caches_t = _da_sample(ps, caches_t, d_b, SAMPLE_PAD, t_new)
    h1_s, xn_s, idx_s, gate_s = _mix(ps, og_s, [o_s], [], hs, prm, TOKEN_TILE)

    n_p, n_s = hp.shape[0], hs.shape[0]
    n_rows = (n_p + n_s) * TOP_K + N_EXPERTS * EXPERT_TILE
    route_tile = max(c for c in range(TOKEN_TILE, ROUTE_TILE_MAX + 1, TOKEN_TILE)
                     if (n_p + n_s) % c == 0)
    dest, block_e, nused = _route(jnp.concatenate([idx_p, idx_s], axis=0), route_tile, EXPERT_TILE,
                                  n_rows // EXPERT_TILE)
    dest3 = dest[:, :TOP_K].reshape(-1, 1, TOKEN_TILE * TOP_K)
    dest3_p, dest3_s = dest3[:n_p // TOKEN_TILE], dest3[n_p // TOKEN_TILE:]
    block_e, nused = block_e[:n_rows // EXPERT_TILE, 0], nused[0, :1]
    xs = _dispatch(dest3, xn_p, xn_s, block_e, nused, n_rows, EXPERT_TILE)
    ys = _experts(block_e, nused, xs, w_exp_in[l], prm["b_exp_in"], w_exp_out[l], prm["b_exp_out"],
                  EXPERT_TILE)
    out_p = _final(dest3_p, h1_p, gate_p, p_prompt[l].reshape(n_p, PLE_DIM), ys, prm)
    out_s = _final(dest3_s, h1_s, gate_s, jnp.pad(p_sample[l], pad_t).reshape(n_s, PLE_DIM), ys, prm)

    y_prompt = out_p.reshape(n_b, seq, D_MODEL)
    y_sample = out_s.reshape(d_b, SAMPLE_PAD, D_MODEL)[:, :t_new]
    outs_p, outs_s = [], []
    for g in range(N_DA_GROUPS):
        win, dil = min(DA_WINDOWS[g], seq), DA_DILATIONS[g]
        ck = (COL_DA + 3 * g + 1) * PROJ_TN
        n_tiles = -(-win // PROJ_ROW_TILE)
        kv_p = pp.reshape(n_b, seq // PROJ_ROW_TILE, dil, PROJ_ROW_TILE // dil, PROJ_WIDTH)
        kv_p = kv_p[:, -n_tiles:, :, :, ck:ck + 2 * DA_WIDTH]
        kv_p = jnp.transpose(kv_p, (0, 1, 3, 2, 4)).reshape(n_b, n_tiles * PROJ_ROW_TILE, -1)
        outs_p.append(kv_p[:, -win:].astype(F32).reshape(1, n_b, win, 2, DA_HEADS, DA_HEAD_DIM))
        outs_s.append(jnp.transpose(new_caches_t[g], (0, 4, 1, 2, 3))[None])
    return (y_prompt, y_sample, state_p[None], state_s[None], outs_p[0], outs_s[0], outs_p[1],
            outs_s[1], outs_p[2], outs_s[2])
```
